```python
import jax, jax.numpy as jnp
from jax import lax
import numpy as np

D_MODEL = 1024
BATCH = 2
SEQ = 8192
DEPTH = 1

HEAD_DIM = 64
NSA_HEADS = 8
NSA_KV_HEADS = 2
NSA_GROUP = NSA_HEADS // NSA_KV_HEADS
CMP_STRIDE = 16
CMP_LEN = 2 * CMP_STRIDE
CMP_HIDDEN = 256
SEL_BLOCK = 64
N_SEL = 16
WINDOW = 512
MOBA_HEADS = 8
MOBA_BLOCK = 256
MOBA_TOPK = 3
N_EXPERTS = 256
TOP_K = 8
N_GROUPS = 8
TOPK_GROUPS = 4
D_EXPERT = 256
D_SHARED = 256
ROUTE_SCALE = 2.5
MOE_ROW_BLOCK = 128

Q_CHUNK = 64
RMS_EPS = 1e-6

NSA_Q = NSA_HEADS * HEAD_DIM
NSA_KV = NSA_KV_HEADS * HEAD_DIM
NSA_GATES = NSA_HEADS * 3
MOBA_W = MOBA_HEADS * HEAD_DIM
IN_SIZES = [NSA_Q, 6 * NSA_KV, NSA_GATES, 3 * MOBA_W, D_MODEL, D_MODEL]
IN_COLS = sum(IN_SIZES)

kernel_name = "hybrid_nsa_moba_moe_adaln_block"


def alibi_slopes():
    n = NSA_HEADS + MOBA_HEADS
    s = 2.0 ** (-8.0 * np.arange(1, n + 1) / n)
    return jnp.asarray(s[0::2], jnp.float32), jnp.asarray(s[1::2], jnp.float32)


def rmsnorm(x, g):
    xf = x.astype(jnp.float32)
    y = xf * lax.rsqrt(jnp.mean(xf * xf, axis=-1, keepdims=True) + RMS_EPS)
    return (y * g.astype(jnp.float32)).astype(x.dtype)


def masked_softmax(s, mask):
    s = jnp.where(mask, s, -jnp.inf)
    m = jnp.max(s, axis=-1, keepdims=True)
    m = jnp.where(jnp.isfinite(m), m, 0.0)
    e = jnp.exp(s - m)
    return e / jnp.maximum(jnp.sum(e, axis=-1, keepdims=True), 1e-30)


def compress_kv(k, pe, w1, b1, w2):
    b_, h_, s_, dh = k.shape
    k16 = k.reshape(b_, h_, s_ // CMP_STRIDE, CMP_STRIDE, dh)
    blocks = jnp.concatenate([k16[:, :, :-1], k16[:, :, 1:]], axis=3) + pe
    flat = blocks.reshape(b_, h_, -1, CMP_LEN * dh)
    return jax.nn.gelu(flat @ w1 + b1) @ w2


_gather_blocks = jax.vmap(jax.vmap(lambda blk, ix: blk[ix]))


def nsa_attention(q, k_c, v_c, k_s, v_s, k_w, v_w, gates,
                  pe_k, w1_k, b1_k, w2_k, pe_v, w1_v, b1_v, w2_v, slopes):
    b_, hkv, g_, s_, dh = q.shape
    scale = dh ** -0.5
    kc = compress_kv(k_c, pe_k, w1_k, b1_k, w2_k)
    vc = compress_kv(v_c, pe_v, w1_v, b1_v, w2_v)
    nc = kc.shape[2]
    c_end = jnp.arange(nc) * CMP_STRIDE + CMP_LEN - 1
    c_mid = jnp.arange(nc, dtype=jnp.float32) * CMP_STRIDE + (CMP_LEN - 1) / 2
    ns = s_ // SEL_BLOCK
    n_sel = min(N_SEL, ns)
    ratio = SEL_BLOCK // CMP_STRIDE
    span = CMP_LEN // CMP_STRIDE
    jj = np.arange(ns)
    imp_idx = jnp.asarray((ratio * jj[:, None, None] + np.arange(ratio)[None, :, None]
                           - np.arange(span)[None, None, :] + 1).reshape(ns, -1))
    k_sb = k_s.reshape(b_, hkv, ns, SEL_BLOCK, dh)
    v_sb = v_s.reshape(b_, hkv, ns, SEL_BLOCK, dh)
    padw = ((0, 0), (0, 0), (WINDOW, 0), (0, 0))
    k_wp = jnp.pad(k_w, padw)
    v_wp = jnp.pad(v_w, padw)
    sl = slopes.reshape(hkv, g_)[None, :, :, None, None]
    jb = jnp.arange(ns)

    def chunk(s0):
        t = s0 + jnp.arange(Q_CHUNK)
        tf = t.astype(jnp.float32)
        qc = lax.dynamic_slice_in_dim(q, s0, Q_CHUNK, axis=3) * scale
        sc = jnp.einsum('bhgqd,bhcd->bhgqc', qc, kc).astype(jnp.float32) \
            - sl * (tf[:, None] - c_mid[None, :])
        p_c = masked_softmax(sc, c_end[None, :] <= t[:, None])
        o_c = jnp.einsum('bhgqc,bhcd->bhgqd', p_c.astype(vc.dtype), vc)
        p_grp = jnp.pad(p_c.sum(axis=2), ((0, 0), (0, 0), (0, 0), (1, 1)))
        imp = jnp.take(p_grp, imp_idx, axis=-1).sum(-1)
        cur = t // SEL_BLOCK
        forced = (jb[None, :] == 0) | (jb[None, :] == cur[:, None]) | (jb[None, :] == cur[:, None] - 1)
        imp = jnp.where(forced, jnp.inf, imp)
        imp = jnp.where(jb[None, :] > cur[:, None], -jnp.inf, imp)
        _, sel = lax.top_k(imp, n_sel)
        ks = _gather_blocks(k_sb, sel).reshape(b_, hkv, Q_CHUNK, n_sel * SEL_BLOCK, dh)
        vs = _gather_blocks(v_sb, sel).reshape(b_, hkv, Q_CHUNK, n_sel * SEL_BLOCK, dh)
        pos = (sel[..., None] * SEL_BLOCK + jnp.arange(SEL_BLOCK)).reshape(b_, hkv, Q_CHUNK, -1)
        dist = (t[:, None] - pos)[:, :, None]
        ss = jnp.einsum('bhgqd,bhqkd->bhgqk', qc, ks).astype(jnp.float32) - sl * dist.astype(jnp.float32)
        p_s = masked_softmax(ss, dist >= 0)
        o_s = jnp.einsum('bhgqk,bhqkd->bhgqd', p_s.astype(vs.dtype), vs)
        kw = lax.dynamic_slice_in_dim(k_wp, s0, Q_CHUNK + WINDOW, axis=2)
        vw = lax.dynamic_slice_in_dim(v_wp, s0, Q_CHUNK + WINDOW, axis=2)
        posw = s0 - WINDOW + jnp.arange(Q_CHUNK + WINDOW)
        dw = t[:, None] - posw[None, :]
        sw = jnp.einsum('bhgqd,bhkd->bhgqk', qc, kw).astype(jnp.float32) - sl * dw.astype(jnp.float32)
        p_w = masked_softmax(sw, (dw >= 0) & (dw < WINDOW) & (posw[None, :] >= 0))
        o_w = jnp.einsum('bhgqk,bhkd->bhgqd', p_w.astype(vw.dtype), vw)
        g = lax.dynamic_slice_in_dim(gates, s0, Q_CHUNK, axis=3)
        return g[..., 0:1] * o_c + g[..., 1:2] * o_s + g[..., 2:3] * o_w

    out = lax.map(chunk, jnp.arange(s_ // Q_CHUNK) * Q_CHUNK)
    return out.transpose(1, 0, 4, 2, 3, 5).reshape(b_, s_, hkv * g_ * dh)


def moba_attention(q, k, v, slopes):
    b_, h_, s_, dh = q.shape
    scale = dh ** -0.5
    nb = -(-s_ // MOBA_BLOCK)
    pad = ((0, 0), (0, 0), (0, nb * MOBA_BLOCK - s_), (0, 0))
    kp = jnp.pad(k, pad)
    vp = jnp.pad(v, pad)
    kb = kp.reshape(b_, h_, nb, MOBA_BLOCK, dh)
    vb = vp.reshape(b_, h_, nb, MOBA_BLOCK, dh)
    kmean = jnp.mean(kb.astype(jnp.float32), axis=3).astype(k.dtype)
    n_top = min(MOBA_TOPK, nb - 1)
    sl = slopes[None, :, None, None]

    def chunk(s0):
        t = s0 + jnp.arange(Q_CHUNK)
        qc = lax.dynamic_slice_in_dim(q, s0, Q_CHUNK, axis=2) * scale
        cur = t // MOBA_BLOCK
        start = (s0 // MOBA_BLOCK) * MOBA_BLOCK
        ko = lax.dynamic_slice_in_dim(kp, start, MOBA_BLOCK, axis=2)
        vo = lax.dynamic_slice_in_dim(vp, start, MOBA_BLOCK, axis=2)
        d_o = t[:, None] - (start + jnp.arange(MOBA_BLOCK))[None, :]
        s_o = jnp.einsum('bhqd,bhkd->bhqk', qc, ko).astype(jnp.float32) - sl * d_o.astype(jnp.float32)
        m_o = jnp.broadcast_to(d_o >= 0, s_o.shape)
        if n_top > 0:
            gate = jnp.einsum('bhqd,bhnd->bhqn', qc, kmean).astype(jnp.float32)
            past = jnp.arange(nb)[None, :] < cur[:, None]
            _, sel = lax.top_k(jnp.where(past, gate, -jnp.inf), n_top)
            ok = jnp.repeat(sel < cur[:, None], MOBA_BLOCK, axis=-1)
            ks = _gather_blocks(kb, sel).reshape(b_, h_, Q_CHUNK, n_top * MOBA_BLOCK, dh)
            vs = _gather_blocks(vb, sel).reshape(b_, h_, Q_CHUNK, n_top * MOBA_BLOCK, dh)
            pos_s = (sel[..., None] * MOBA_BLOCK + jnp.arange(MOBA_BLOCK)).reshape(b_, h_, Q_CHUNK, -1)
            s_s = jnp.einsum('bhqd,bhqkd->bhqk', qc, ks).astype(jnp.float32) \
                - sl * (t[:, None] - pos_s).astype(jnp.float32)
            p = masked_softmax(jnp.concatenate([s_s, s_o], axis=-1),
                               jnp.concatenate([ok, m_o], axis=-1)).astype(v.dtype)
            n_s = n_top * MOBA_BLOCK
            return jnp.einsum('bhqk,bhqkd->bhqd', p[..., :n_s], vs) + \
                jnp.einsum('bhqk,bhkd->bhqd', p[..., n_s:], vo)
        p = masked_softmax(s_o, m_o).astype(v.dtype)
        return jnp.einsum('bhqk,bhkd->bhqd', p, vo)

    out = lax.map(chunk, jnp.arange(s_ // Q_CHUNK) * Q_CHUNK)
    return out.transpose(1, 0, 3, 2, 4).reshape(b_, s_, h_ * dh)


def hybrid_mixer(h, w_in, pe_k, w1_k, b1_k, w2_k, pe_v, w1_v, b1_v, w2_v,
                 w_up_a, w_up_b, w_out, slopes_a, slopes_b):
    b_, s_, _ = h.shape
    proj = h @ w_in
    q_a, kv_a, g_a, qkv_b, gate_a, gate_b = jnp.split(proj, [int(i) for i in np.cumsum(IN_SIZES)[:-1]], axis=-1)
    q_a = q_a.reshape(b_, s_, NSA_KV_HEADS, NSA_GROUP, HEAD_DIM).transpose(0, 2, 3, 1, 4)
    kv_a = kv_a.reshape(b_, s_, 6, NSA_KV_HEADS, HEAD_DIM).transpose(2, 0, 3, 1, 4)
    g_a = jax.nn.sigmoid(g_a).reshape(b_, s_, NSA_KV_HEADS, NSA_GROUP, 3).transpose(0, 2, 3, 1, 4)
    o_a = nsa_attention(q_a, kv_a[0], kv_a[1], kv_a[2], kv_a[3], kv_a[4], kv_a[5], g_a,
                        pe_k, w1_k, b1_k, w2_k, pe_v, w1_v, b1_v, w2_v, slopes_a)
    qkv_b = qkv_b.reshape(b_, s_, 3, MOBA_HEADS, HEAD_DIM).transpose(2, 0, 3, 1, 4)
    o_b = moba_attention(qkv_b[0], qkv_b[1], qkv_b[2], slopes_b)
    merged = jax.nn.sigmoid(gate_a) * (o_a @ w_up_a) + jax.nn.sigmoid(gate_b) * (o_b @ w_up_b)
    return merged @ w_out


def route(h2d, w_router, router_bias):
    t_ = h2d.shape[0]
    s = jax.nn.sigmoid((h2d @ w_router).astype(jnp.float32))
    sb = s + router_bias.astype(jnp.float32)
    gscore = lax.top_k(sb.reshape(t_, N_GROUPS, -1), 2)[0].sum(-1)
    _, gidx = lax.top_k(gscore, TOPK_GROUPS)
    gmask = jnp.any(gidx[..., None] == jnp.arange(N_GROUPS), axis=1)
    emask = jnp.repeat(gmask, N_EXPERTS // N_GROUPS, axis=1)
    _, eidx = lax.top_k(jnp.where(emask, sb, -jnp.inf), TOP_K)
    w = jnp.take_along_axis(s, eidx, axis=1)
    w = w / jnp.sum(w, axis=-1, keepdims=True) * ROUTE_SCALE
    return eidx, w


def routed_experts(h2d, eidx, wts, w_gate, w_up, w_down):
    t_, d_ = h2d.shape
    a_ = t_ * TOP_K
    blk = MOE_ROW_BLOCK
    nb = -(-(a_ + N_EXPERTS * (blk - 1)) // blk)
    flat_e = eidx.reshape(-1)
    flat_tok = jnp.repeat(jnp.arange(t_, dtype=jnp.int32), TOP_K)
    flat_w = wts.reshape(-1)
    order = jnp.argsort(flat_e)
    e_sorted = flat_e[order]
    counts = jnp.bincount(flat_e, length=N_EXPERTS)
    start = jnp.cumsum(counts) - counts
    padded = (counts + blk - 1) // blk * blk
    pad_end = jnp.cumsum(padded)
    pad_start = pad_end - padded
    dest = pad_start[e_sorted] + jnp.arange(a_) - start[e_sorted]
    row_tok = jnp.full((nb * blk,), t_, jnp.int32).at[dest].set(flat_tok[order])
    row_w = jnp.zeros((nb * blk,), h2d.dtype).at[dest].set(flat_w[order].astype(h2d.dtype))
    blk_e = jnp.minimum(jnp.searchsorted(pad_end, jnp.arange(nb) * blk, side='right'), N_EXPERTS - 1)
    x_pad = jnp.concatenate([h2d, jnp.zeros((1, d_), h2d.dtype)], axis=0)

    def step(acc, inp):
        tok, w, e = inp
        xb = x_pad[tok]
        hb = jax.nn.silu(xb @ w_gate[e]) * (xb @ w_up[e])
        return acc.at[tok].add((hb @ w_down[e]) * w[:, None]), None

    acc, _ = lax.scan(step, jnp.zeros((t_ + 1, d_), h2d.dtype),
                      (row_tok.reshape(nb, blk), row_w.reshape(nb, blk), blk_e))
    return acc[:t_]


def moe_ffn(h, w_router, router_bias, w_e_gate, w_e_up, w_e_down, w_s_gate, w_s_up, w_s_down):
    b_, s_, d_ = h.shape
    h2 = h.reshape(-1, d_)
    eidx, wts = route(h2, w_router, router_bias)
    y = routed_experts(h2, eidx, wts, w_e_gate, w_e_up, w_e_down)
    y = y + (jax.nn.silu(h2 @ w_s_gate) * (h2 @ w_s_up)) @ w_s_down
    return y.reshape(b_, s_, d_)


def setup_inputs(seed: int = 0) -> dict:
    key = jax.random.key(seed)
    ks = jax.random.split(key, 32)
    L, D = DEPTH, D_MODEL

    def nrm(k, shape, scale):
        return jax.random.normal(k, shape, jnp.float32) * scale

    return {
        "x": nrm(ks[0], (BATCH, SEQ, D), 1.0),
        "c": nrm(ks[1], (BATCH, D), 1.0),
        "w_ada": nrm(ks[2], (L, D, 6 * D), 0.5 * D ** -0.5),
        "b_ada": nrm(ks[3], (L, 6 * D), 0.01),
        "norm1_g": 1.0 + nrm(ks[4], (L, D), 0.02),
        "w_in": nrm(ks[5], (L, D, IN_COLS), D ** -0.5),
        "cmp_pe_k": nrm(ks[6], (L, CMP_LEN, HEAD_DIM), 0.1),
        "cmp_w1_k": nrm(ks[7], (L, CMP_LEN * HEAD_DIM, CMP_HIDDEN), (CMP_LEN * HEAD_DIM) ** -0.5),
        "cmp_b1_k": nrm(ks[8], (L, CMP_HIDDEN), 0.01),
        "cmp_w2_k": nrm(ks[9], (L, CMP_HIDDEN, HEAD_DIM), CMP_HIDDEN ** -0.5),
        "cmp_pe_v": nrm(ks[10], (L, CMP_LEN, HEAD_DIM), 0.1),
        "cmp_w1_v": nrm(ks[11], (L, CMP_LEN * HEAD_DIM, CMP_HIDDEN), (CMP_LEN * HEAD_DIM) ** -0.5),
        "cmp_b1_v": nrm(ks[12], (L, CMP_HIDDEN), 0.01),
        "cmp_w2_v": nrm(ks[13], (L, CMP_HIDDEN, HEAD_DIM), CMP_HIDDEN ** -0.5),
        "w_up_a": nrm(ks[14], (L, NSA_Q, D), NSA_Q ** -0.5),
        "w_up_b": nrm(ks[15], (L, MOBA_W, D), MOBA_W ** -0.5),
        "w_out": nrm(ks[16], (L, D, D), D ** -0.5),
        "norm2_g": 1.0 + nrm(ks[17], (L, D), 0.02),
        "w_router": nrm(ks[18], (L, D, N_EXPERTS), D ** -0.5),
        "router_bias": nrm(ks[19], (L, N_EXPERTS), 0.01),
        "w_e_gate": nrm(ks[20], (L, N_EXPERTS, D, D_EXPERT), D ** -0.5),
        "w_e_up": nrm(ks[21], (L, N_EXPERTS, D, D_EXPERT), D ** -0.5),
        "w_e_down": nrm(ks[22], (L, N_EXPERTS, D_EXPERT, D), D_EXPERT ** -0.5),
        "w_s_gate": nrm(ks[23], (L, D, D_SHARED), D ** -0.5),
        "w_s_up": nrm(ks[24], (L, D, D_SHARED), D ** -0.5),
        "w_s_down": nrm(ks[25], (L, D_SHARED, D), D_SHARED ** -0.5),
        "final_g": 1.0 + nrm(ks[26], (D,), 0.02),
    }


def reference(x, c, w_ada, b_ada, norm1_g, w_in, cmp_pe_k, cmp_w1_k, cmp_b1_k, cmp_w2_k,
              cmp_pe_v, cmp_w1_v, cmp_b1_v, cmp_w2_v, w_up_a, w_up_b, w_out, norm2_g,
              w_router, router_bias, w_e_gate, w_e_up, w_e_down, w_s_gate, w_s_up, w_s_down,
              final_g):
    slopes_a, slopes_b = alibi_slopes()
    for l in range(DEPTH):
        ada = jax.nn.silu(c) @ w_ada[l] + b_ada[l]
        sh1, sc1, g1, sh2, sc2, g2 = [a[:, None, :] for a in jnp.split(ada, 6, axis=-1)]
        h = rmsnorm(x, norm1_g[l]) * (1 + sc1) + sh1
        mix = hybrid_mixer(h, w_in[l], cmp_pe_k[l], cmp_w1_k[l], cmp_b1_k[l], cmp_w2_k[l],
                           cmp_pe_v[l], cmp_w1_v[l], cmp_b1_v[l], cmp_w2_v[l],
                           w_up_a[l], w_up_b[l], w_out[l], slopes_a, slopes_b)
        x = x + g1 * mix
        h = rmsnorm(x, norm2_g[l]) * (1 + sc2) + sh2
        x = x + g2 * moe_ffn(h, w_router[l], router_bias[l], w_e_gate[l], w_e_up[l], w_e_down[l],
                             w_s_gate[l], w_s_up[l], w_s_down[l])
    return rmsnorm(x, final_g)
```

```python
import functools

import numpy as np
import jax
import jax.numpy as jnp
from jax import lax
from jax.experimental import pallas as pl
from jax.experimental.pallas import tpu as pltpu

D_MODEL = 1024
HEAD_DIM = 64
NSA_HEADS = 8
NSA_KV_HEADS = 2
NSA_GROUP = NSA_HEADS // NSA_KV_HEADS
CMP_STRIDE = 16
CMP_LEN = 2 * CMP_STRIDE
CMP_HIDDEN = 256
SEL_BLOCK = 64
N_SEL = 16
WINDOW = 512
MOBA_HEADS = 8
MOBA_BLOCK = 256
MOBA_TOPK = 3
N_EXPERTS = 256
TOP_K = 8
N_GROUPS = 8
TOPK_GROUPS = 4
D_EXPERT = 256
D_SHARED = 256
ROUTE_SCALE = 2.5
RMS_EPS = 1e-6

NSA_Q = NSA_HEADS * HEAD_DIM
NSA_KV = NSA_KV_HEADS * HEAD_DIM
NSA_GATES = NSA_HEADS * 3
MOBA_W = MOBA_HEADS * HEAD_DIM

LANES = 128
VMEM_LIMIT = 52 * 1024 * 1024

SEG_QA = (0, NSA_Q)
SEG_KVA = (SEG_QA[1], SEG_QA[1] + 6 * NSA_KV)
SEG_GA = (SEG_KVA[1], SEG_KVA[1] + LANES)
SEG_QKVB = (SEG_GA[1], SEG_GA[1] + 3 * MOBA_W)
SEG_GATES = (SEG_QKVB[1], SEG_QKVB[1] + 2 * D_MODEL)
IN_COLS_PACKED = SEG_GATES[1]

BIG = float(2 ** 30)
NEG = -1e30
FEAT0 = HEAD_DIM
MOBA_SEL0 = 96

TM_ROWS = 256
NSA_TQ = 128
NSA_TK = 512
MOBA_TQ = MOBA_BLOCK
MOE_BM = 256

F32 = jnp.float32
BF16 = jnp.bfloat16


def _cparams(sem):
    return pltpu.CompilerParams(dimension_semantics=sem, vmem_limit_bytes=VMEM_LIMIT)


def _dot_t(a, b):
    return lax.dot_general(a, b, (((1,), (1,)), ((), ())), preferred_element_type=F32)


def _dot(a, b):
    return jnp.dot(a, b, preferred_element_type=F32)


def _split3(a):
    hi = a.astype(BF16)
    r = a - hi.astype(F32)
    mid = r.astype(BF16)
    lo = (r - mid.astype(F32)).astype(BF16)
    return hi, mid, lo


def _ada_kernel(c_ref, w_ref, b_ref, o_ref):
    c = c_ref[...]
    a = c * jax.nn.sigmoid(c)
    a1, a2, a3 = _split3(a)
    w1, w2, w3 = _split3(w_ref[...])
    acc = _dot(a1, w1) + (_dot(a1, w2) + _dot(a2, w1)) + (_dot(a1, w3) + _dot(a2, w2) + _dot(a3, w1))
    o_ref[...] = acc + b_ref[...]


def _ada(c, w, b):
    bsz, d = c.shape
    n = w.shape[1]
    cp = jnp.zeros((8, d), F32).at[:bsz].set(c)
    out = pl.pallas_call(
        _ada_kernel,
        grid=(n // d,),
        in_specs=[pl.BlockSpec((8, d), lambda j: (0, 0)),
                  pl.BlockSpec((d, d), lambda j: (0, j)),
                  pl.BlockSpec((1, d), lambda j: (0, j))],
        out_specs=pl.BlockSpec((8, d), lambda j: (0, j)),
        out_shape=jax.ShapeDtypeStruct((8, n), F32),
        compiler_params=_cparams(("arbitrary",)),
        name="ada",
    )(cp, w, b.reshape(1, n))
    return out[:bsz]


def _modulated_rmsnorm(x, g, sc, sh):
    y = x * lax.rsqrt(jnp.mean(x * x, axis=-1, keepdims=True) + RMS_EPS)
    return (y * g) * (1.0 + sc) + sh


def _inproj_kernel(x_ref, sc_ref, sh_ref, g_ref, w_ref, qa_ref, kva_ref, ga_ref, qkvb_ref, gates_ref):
    h = _modulated_rmsnorm(x_ref[...], g_ref[...], sc_ref[0], sh_ref[0]).astype(BF16)
    scale = HEAD_DIM ** -0.5
    qa_ref[...] = (_dot(h, w_ref[:, SEG_QA[0]:SEG_QA[1]]) * scale).astype(BF16)
    kva_ref[...] = _dot(h, w_ref[:, SEG_KVA[0]:SEG_KVA[1]]).astype(BF16)
    ga_ref[...] = jax.nn.sigmoid(_dot(h, w_ref[:, SEG_GA[0]:SEG_GA[1]]))
    qb0 = SEG_QKVB[0]
    qkvb_ref[:, :MOBA_W] = (_dot(h, w_ref[:, qb0:qb0 + MOBA_W]) * scale).astype(BF16)
    qkvb_ref[:, MOBA_W:] = _dot(h, w_ref[:, qb0 + MOBA_W:SEG_QKVB[1]]).astype(BF16)
    gates_ref[...] = jax.nn.sigmoid(_dot(h, w_ref[:, SEG_GATES[0]:SEG_GATES[1]])).astype(BF16)


def _inproj(x2, sc, sh, g, w_packed, seq):
    t, d = x2.shape
    tm = TM_ROWS
    per_b = seq // tm
    row = lambda i: (i, 0)
    bat = lambda i: (i // per_b, 0, 0)
    widths = (NSA_Q, 6 * NSA_KV, LANES, 3 * MOBA_W, 2 * D_MODEL)
    dts = (BF16, BF16, F32, BF16, BF16)
    return pl.pallas_call(
        _inproj_kernel,
        grid=(t // tm,),
        in_specs=[pl.BlockSpec((tm, d), row),
                  pl.BlockSpec((1, 1, d), bat),
                  pl.BlockSpec((1, 1, d), bat),
                  pl.BlockSpec((1, d), lambda i: (0, 0)),
                  pl.BlockSpec((d, IN_COLS_PACKED), lambda i: (0, 0))],
        out_specs=[pl.BlockSpec((tm, w), row) for w in widths],
        out_shape=[jax.ShapeDtypeStruct((t, w), dt) for w, dt in zip(widths, dts)],
        compiler_params=_cparams(("arbitrary",)),
        name="inproj",
    )(x2, sc, sh, g, w_packed)


def _pack_w_in(w_in):
    sizes = [NSA_Q, 6 * NSA_KV, NSA_GATES, 3 * MOBA_W, D_MODEL, D_MODEL]
    offs = np.cumsum([0] + sizes)
    parts = [w_in[:, offs[i]:offs[i + 1]] for i in range(6)]
    parts[2] = jnp.pad(parts[2], ((0, 0), (0, LANES - NSA_GATES)))
    return jnp.concatenate(parts, axis=1).astype(BF16)


def _compress_kernel(x_ref, pe_ref, w1_ref, b1_ref, w2_ref, o_ref):
    x = x_ref[0, 0]
    half = CMP_STRIDE * HEAD_DIM
    w1 = w1_ref[0].astype(BF16)
    first = _dot(x, w1[:half])
    second = _dot(x, w1[half:])
    const = _dot(pe_ref[0].astype(BF16), w1)[0:1] + b1_ref[0]
    n = x.shape[0]
    hidden = first + pltpu.roll(second, n - 1, 0) + const
    o_ref[0, 0] = _dot(jax.nn.gelu(hidden).astype(BF16), w2_ref[0].astype(BF16))


def _compress(x, pe, w1, b1, w2):
    two, bh, n, kd = x.shape
    return pl.pallas_call(
        _compress_kernel,
        grid=(two, bh),
        in_specs=[pl.BlockSpec((1, 1, n, kd), lambda a, b: (a, b, 0, 0)),
                  pl.BlockSpec((1,) + pe.shape[1:], lambda a, b: (a, 0, 0)),
                  pl.BlockSpec((1,) + w1.shape[1:], lambda a, b: (a, 0, 0)),
                  pl.BlockSpec((1,) + b1.shape[1:], lambda a, b: (a, 0, 0)),
                  pl.BlockSpec((1,) + w2.shape[1:], lambda a, b: (a, 0, 0))],
        out_specs=pl.BlockSpec((1, 1, n, HEAD_DIM), lambda a, b: (a, b, 0, 0)),
        out_shape=jax.ShapeDtypeStruct((two, bh, n, HEAD_DIM), F32),
        compiler_params=_cparams(("arbitrary", "arbitrary")),
        name="compress",
    )(x, pe, w1, b1, w2)


def _alibi_slopes():
    n = NSA_HEADS + MOBA_HEADS
    s = (2.0 ** (-8.0 * np.arange(1, n + 1) / n)).astype(np.float32)
    return s[0::2], s[1::2]


def _slope_features(slopes):
    s = jnp.asarray(slopes, F32)
    h1, h2, h3 = _split3(s)
    return jnp.stack([h1, h2, h3, h1, h2, h3], axis=-1)


def _pos_features(pos_hi, pos_lo):
    hi = np.asarray(pos_hi, np.float32)
    lo = np.asarray(pos_lo, np.float32)
    return jnp.asarray(np.stack([hi, hi, hi, lo, lo, lo], axis=-1), BF16)


def _flash_step(s, v, m_sc, l_sc, acc_sc):
    m_prev = m_sc[...]
    m_new = jnp.maximum(m_prev, jnp.max(s, axis=-1, keepdims=True))
    alpha = jnp.exp(m_prev - m_new)
    p = jnp.exp(s - m_new)
    l_sc[...] = alpha * l_sc[...] + jnp.sum(p, axis=-1, keepdims=True)
    acc_sc[...] = alpha * acc_sc[...] + _dot(p.astype(BF16), v)
    m_sc[...] = m_new


def _flash_init(m_sc, l_sc, acc_sc):
    m_sc[...] = jnp.full(m_sc.shape, NEG, F32)
    l_sc[...] = jnp.zeros(l_sc.shape, F32)
    acc_sc[...] = jnp.zeros(acc_sc.shape, F32)


def _nsa_kernel(q_ref, kc_ref, vc_ref, ks_ref, vs_ref, kw_ref, vw_ref, g_ref, impm_ref, o_ref,
                m_sc, l_sc, acc_sc, *, tq, tk):
    t0 = pl.program_id(2) * tq
    rows = NSA_GROUP * tq
    q0 = q_ref[0, 0].reshape(rows, LANES)
    t_q = t0 + lax.broadcasted_iota(jnp.int32, (tq, 1), 0)
    t_r = jnp.concatenate([t_q] * NSA_GROUP, axis=0)

    nc = kc_ref.shape[2]
    sc = _dot_t(q0, kc_ref[0, 0])
    c_end = lax.broadcasted_iota(jnp.int32, (1, nc), 1) * CMP_STRIDE + (CMP_LEN - 1)
    sc = jnp.where(c_end <= t_r, sc, -jnp.inf)
    m = jnp.max(sc, axis=-1, keepdims=True)
    m = jnp.where(m == -jnp.inf, 0.0, m)
    e = jnp.exp(sc - m)
    p_c = e / jnp.maximum(jnp.sum(e, axis=-1, keepdims=True), 1e-30)
    o_c = _dot(p_c.astype(BF16), vc_ref[0, 0])

    p_grp = p_c[0:tq]
    for g in range(1, NSA_GROUP):
        p_grp = p_grp + p_c[g * tq:(g + 1) * tq]
    p1, p2, p3 = _split3(p_grp)
    impm = impm_ref[...]
    imp = _dot(p1, impm) + _dot(p2, impm) + _dot(p3, impm)
    ns = imp.shape[1]
    jb = lax.broadcasted_iota(jnp.int32, (1, ns), 1).astype(F32)
    cur = jnp.right_shift(t_q, SEL_BLOCK.bit_length() - 1).astype(F32)
    forced = (jb == 0.0) | (jb == cur) | (jb == cur - 1.0)
    work = jnp.where(forced, jnp.inf, imp)
    work = jnp.where(jb > cur, -jnp.inf, work)
    selm1 = jnp.full((tq, ns), -1.0, F32)
    for _ in range(N_SEL):
        mx = jnp.max(work, axis=-1, keepdims=True)
        idx = jnp.min(jnp.where(work == mx, jb, float(ns)), axis=-1, keepdims=True)
        pick = jb == idx
        selm1 = jnp.where(pick, 0.0, selm1)
        work = jnp.where(pick, -jnp.inf, work)
    selm1 = selm1.astype(BF16)
    q_full = jnp.concatenate([q0, jnp.concatenate([selm1] * NSA_GROUP, axis=0)], axis=1)

    _flash_init(m_sc, l_sc, acc_sc)
    kt_diag = t0 // tk

    def body(kt, carry):
        k0 = pl.multiple_of(kt * tk, tk)
        s = _dot_t(q_full, ks_ref[0, 0, pl.ds(k0, tk), :])
        _flash_step(s, vs_ref[0, 0, pl.ds(k0, tk), :], m_sc, l_sc, acc_sc)
        return carry

    lax.fori_loop(0, kt_diag, body, 0)
    k0 = pl.multiple_of(kt_diag * tk, tk)
    s = _dot_t(q_full, ks_ref[0, 0, pl.ds(k0, tk), :])
    pos = k0 + lax.broadcasted_iota(jnp.int32, (1, tk), 1)
    s = jnp.where(pos <= t_r, s, NEG)
    _flash_step(s, vs_ref[0, 0, pl.ds(k0, tk), :], m_sc, l_sc, acc_sc)
    o_s = acc_sc[...] / jnp.maximum(l_sc[...], 1e-30)

    band = WINDOW + tq
    w0 = pl.multiple_of(jnp.maximum(t0 - WINDOW, 0), tq)
    sw = _dot_t(q0, kw_ref[0, 0, pl.ds(w0, band), :])
    dist = (t_r - w0) - lax.broadcasted_iota(jnp.int32, (1, band), 1)
    in_window = lax.bitcast_convert_type(dist, jnp.uint32) < jnp.uint32(WINDOW)
    sw = jnp.where(in_window, sw, NEG)
    ew = jnp.exp(sw - jnp.max(sw, axis=-1, keepdims=True))
    o_w = _dot(ew.astype(BF16), vw_ref[0, 0, pl.ds(w0, band), :]) \
        / jnp.maximum(jnp.sum(ew, axis=-1, keepdims=True), 1e-30)

    gates = g_ref[0, 0]
    outs = []
    for g in range(NSA_GROUP):
        r = slice(g * tq, (g + 1) * tq)
        outs.append(gates[:, 3 * g:3 * g + 1] * o_c[r] + gates[:, 3 * g + 1:3 * g + 2] * o_s[r]
                    + gates[:, 3 * g + 2:3 * g + 3] * o_w[r])
    o_ref[0] = jnp.concatenate(outs, axis=1).astype(BF16)


def _nsa(q_aug, kc_aug, vc, ks_aug, vs, kw_aug, vw, gates, impm):
    bsz, hkv, grp, seq, _ = q_aug.shape
    tq, tk = NSA_TQ, NSA_TK
    nc = kc_aug.shape[2]
    rows = grp * tq
    kv_map = lambda b, h, i: (b, h, 0, 0)
    return pl.pallas_call(
        functools.partial(_nsa_kernel, tq=tq, tk=tk),
        grid=(bsz, hkv, seq // tq),
        in_specs=[pl.BlockSpec((1, 1, grp, tq, LANES), lambda b, h, i: (b, h, 0, i, 0)),
                  pl.BlockSpec((1, 1, nc, LANES), kv_map),
                  pl.BlockSpec((1, 1, nc, HEAD_DIM), kv_map),
                  pl.BlockSpec((1, 1, seq, 2 * LANES), kv_map),
                  pl.BlockSpec((1, 1, seq, HEAD_DIM), kv_map),
                  pl.BlockSpec((1, 1, seq, LANES), kv_map),
                  pl.BlockSpec((1, 1, seq, HEAD_DIM), kv_map),
                  pl.BlockSpec((1, 1, tq, LANES), lambda b, h, i: (b, h, i, 0)),
                  pl.BlockSpec(impm.shape, lambda b, h, i: (0, 0))],
        out_specs=pl.BlockSpec((1, tq, grp * HEAD_DIM), lambda b, h, i: (b, i, h)),
        out_shape=jax.ShapeDtypeStruct((bsz, seq, hkv * grp * HEAD_DIM), BF16),
        scratch_shapes=[pltpu.VMEM((rows, 1), F32), pltpu.VMEM((rows, 1), F32),
                        pltpu.VMEM((rows, HEAD_DIM), F32)],
        compiler_params=_cparams(("arbitrary", "arbitrary", "arbitrary")),
        name="nsa",
    )(q_aug, kc_aug, vc, ks_aug, vs, kw_aug, vw, gates, impm)


def _importance_matrix(nc, ns):
    ratio = SEL_BLOCK // CMP_STRIDE
    span = CMP_LEN // CMP_STRIDE
    m = np.zeros((nc, ns), np.float32)
    for j in range(ns):
        for r in range(ratio):
            for s in range(span):
                c = ratio * j + r - s
                if 0 <= c < nc:
                    m[c, j] += 1.0
    return jnp.asarray(m, BF16)


def _kmean_kernel(k_ref, o_ref):
    k = k_ref[0, 0].astype(F32)
    nb = k.shape[0] // MOBA_BLOCK
    o_ref[0, 0] = jnp.sum(k.reshape(nb, MOBA_BLOCK, HEAD_DIM), axis=1) / MOBA_BLOCK


def _kmean(k):
    bsz, h, seq, dh = k.shape
    nb = seq // MOBA_BLOCK
    return pl.pallas_call(
        _kmean_kernel,
        grid=(bsz, h),
        in_specs=[pl.BlockSpec((1, 1, seq, dh), lambda b, j: (b, j, 0, 0))],
        out_specs=pl.BlockSpec((1, 1, nb, dh), lambda b, j: (b, j, 0, 0)),
        out_shape=jax.ShapeDtypeStruct((bsz, h, nb, dh), F32),
        compiler_params=_cparams(("arbitrary", "arbitrary")),
        name="kmean",
    )(k)


def _moba_kernel(q_ref, km_ref, k_ref, v_ref, o_ref, m_sc, l_sc, acc_sc, *, tq):
    cur = pl.program_id(2)
    t0 = cur * tq
    q0 = q_ref[0, 0]
    gate = _dot_t(q0, km_ref[0, 0])
    lane = lax.broadcasted_iota(jnp.int32, (1, LANES), 1)
    lane_f = lane.astype(F32)
    jb = lane - MOBA_SEL0
    past = lax.bitcast_convert_type(jb, jnp.uint32) < cur.astype(jnp.uint32)
    work = jnp.where(past, gate, -jnp.inf)
    picked = jnp.zeros((tq, LANES), F32)
    for _ in range(MOBA_TOPK):
        mx = jnp.max(work, axis=-1, keepdims=True)
        idx = jnp.min(jnp.where(work == mx, lane_f, float(LANES)), axis=-1, keepdims=True)
        pick = lane_f == idx
        picked = jnp.where(pick, 1.0, picked)
        work = jnp.where(pick, -jnp.inf, work)
    keep = jnp.where(past, picked, 0.0) + jnp.where(jb == cur, 1.0, 0.0)
    q_full = jnp.where(lane >= MOBA_SEL0, (keep - 1.0).astype(BF16), q0)

    _flash_init(m_sc, l_sc, acc_sc)

    def body(kt, carry):
        k0 = pl.multiple_of(kt * tq, tq)
        s = _dot_t(q_full, k_ref[0, 0, pl.ds(k0, tq), :])
        _flash_step(s, v_ref[0, 0, pl.ds(k0, tq), :], m_sc, l_sc, acc_sc)
        return carry

    lax.fori_loop(0, cur, body, 0)
    k0 = pl.multiple_of(t0, tq)
    s = _dot_t(q_full, k_ref[0, 0, pl.ds(k0, tq), :])
    causal = lax.broadcasted_iota(jnp.int32, (1, tq), 1) <= lax.broadcasted_iota(jnp.int32, (tq, 1), 0)
    s = jnp.where(causal, s, NEG)
    _flash_step(s, v_ref[0, 0, pl.ds(k0, tq), :], m_sc, l_sc, acc_sc)
    o_ref[0, 0] = (acc_sc[...] / jnp.maximum(l_sc[...], 1e-30)).astype(BF16)


def _moba(q_aug, km_aug, k_aug, v):
    bsz, h, seq, _ = q_aug.shape
    tq = MOBA_TQ
    kv_map = lambda b, j, i: (b, j, 0, 0)
    return pl.pallas_call(
        functools.partial(_moba_kernel, tq=tq),
        grid=(bsz, h, seq // tq),
        in_specs=[pl.BlockSpec((1, 1, tq, LANES), lambda b, j, i: (b, j, i, 0)),
                  pl.BlockSpec((1, 1, LANES, LANES), kv_map),
                  pl.BlockSpec((1, 1, seq, LANES), kv_map),
                  pl.BlockSpec((1, 1, seq, HEAD_DIM), kv_map)],
        out_specs=pl.BlockSpec((1, 1, tq, HEAD_DIM), lambda b, j, i: (b, j, i, 0)),
        out_shape=jax.ShapeDtypeStruct((bsz, h, seq, HEAD_DIM), BF16),
        scratch_shapes=[pltpu.VMEM((tq, 1), F32), pltpu.VMEM((tq, 1), F32), pltpu.VMEM((tq, HEAD_DIM), F32)],
        compiler_params=_cparams(("arbitrary", "arbitrary", "arbitrary")),
        name="moba",
    )(q_aug, km_aug, k_aug, v)


def _merge_kernel(oa_ref, ob_ref, gates_ref, x_ref, g1_ref, sc_ref, sh_ref, n2_ref, wua_ref, wub_ref, wout_ref,
                  wr_ref, x1_ref, h2_ref, logit_ref):
    ua = _dot(oa_ref[...], wua_ref[...])
    ub = _dot(ob_ref[...], wub_ref[...])
    merged = gates_ref[:, :D_MODEL].astype(F32) * ua + gates_ref[:, D_MODEL:].astype(F32) * ub
    mix = _dot(merged.astype(BF16), wout_ref[...])
    x1 = x_ref[...] + g1_ref[0] * mix
    x1_ref[...] = x1
    h2 = _modulated_rmsnorm(x1, n2_ref[...], sc_ref[0], sh_ref[0])
    h2b = h2.astype(BF16)
    h2_ref[...] = h2b
    h2lo = (h2 - h2b.astype(F32)).astype(BF16)
    w1, w2, _ = _split3(wr_ref[...])
    logit_ref[...] = _dot(h2b, w1) + (_dot(h2b, w2) + _dot(h2lo, w1))


def _merge(oa, ob, gates, x2, g1, sc2, sh2, n2, wua, wub, wout, wr, seq):
    t, d = x2.shape
    tm = TM_ROWS
    per_b = seq // tm
    row = lambda i: (i, 0)
    bat = lambda i: (i // per_b, 0, 0)
    full = lambda i: (0, 0)
    return pl.pallas_call(
        _merge_kernel,
        grid=(t // tm,),
        in_specs=[pl.BlockSpec((tm, oa.shape[1]), row), pl.BlockSpec((tm, ob.shape[1]), row),
                  pl.BlockSpec((tm, 2 * d), row), pl.BlockSpec((tm, d), row),
                  pl.BlockSpec((1, 1, d), bat), pl.BlockSpec((1, 1, d), bat), pl.BlockSpec((1, 1, d), bat),
                  pl.BlockSpec((1, d), full),
                  pl.BlockSpec(wua.shape, full), pl.BlockSpec(wub.shape, full), pl.BlockSpec(wout.shape, full),
                  pl.BlockSpec(wr.shape, full)],
        out_specs=[pl.BlockSpec((tm, d), row), pl.BlockSpec((tm, d), row), pl.BlockSpec((tm, N_EXPERTS), row)],
        out_shape=[jax.ShapeDtypeStruct((t, d), F32), jax.ShapeDtypeStruct((t, d), BF16),
                   jax.ShapeDtypeStruct((t, N_EXPERTS), F32)],
        compiler_params=_cparams(("arbitrary",)),
        name="merge",
    )(oa, ob, gates, x2, g1, sc2, sh2, n2, wua, wub, wout, wr)


def _route_kernel(logit_ref, bias_ref, eidx_ref, wts_ref):
    s = jax.nn.sigmoid(logit_ref[...])
    sb = s + bias_ref[...]
    tm, ne = s.shape
    per_g = ne // N_GROUPS
    lane_i = lax.broadcasted_iota(jnp.int32, (1, ne), 1)
    grp = jnp.right_shift(lane_i, per_g.bit_length() - 1)
    lane = lane_i.astype(F32)

    def lowest_argmax(v):
        mx = jnp.max(v, axis=-1, keepdims=True)
        return mx, jnp.min(jnp.where(v == mx, lane, float(ne)), axis=-1, keepdims=True)

    gscore = []
    for g in range(N_GROUPS):
        vg = jnp.where(grp == g, sb, -jnp.inf)
        m1, i1 = lowest_argmax(vg)
        m2 = jnp.max(jnp.where(lane == i1, -jnp.inf, vg), axis=-1, keepdims=True)
        gscore.append(m1 + m2)
    emask = jnp.zeros((tm, ne), F32)
    for g in range(N_GROUPS):
        rank = jnp.zeros((tm, 1), F32)
        for h in range(N_GROUPS):
            if h == g:
                continue
            ahead = (gscore[h] >= gscore[g]) if h < g else (gscore[h] > gscore[g])
            rank = rank + jnp.where(ahead, 1.0, 0.0)
        emask = jnp.where(grp == g, jnp.where(rank < TOPK_GROUPS, 1.0, 0.0), emask)
    work = jnp.where(emask > 0.5, sb, -jnp.inf)
    out_lane = lax.broadcasted_iota(jnp.int32, (1, LANES), 1)
    eidx = jnp.zeros((tm, LANES), F32)
    wts = jnp.zeros((tm, LANES), F32)
    for k in range(TOP_K):
        _, idx = lowest_argmax(work)
        pick = lane == idx
        wk = jnp.sum(jnp.where(pick, s, 0.0), axis=-1, keepdims=True)
        work = jnp.where(pick, -jnp.inf, work)
        eidx = jnp.where(out_lane == k, idx, eidx)
        wts = jnp.where(out_lane == k, wk, wts)
    eidx_ref[...] = eidx.astype(jnp.int32)
    wts_ref[...] = wts / jnp.sum(wts, axis=-1, keepdims=True) * ROUTE_SCALE


def _route(logits, bias):
    t, ne = logits.shape
    tm = TM_ROWS
    row = lambda i: (i, 0)
    return pl.pallas_call(
        _route_kernel,
        grid=(t // tm,),
        in_specs=[pl.BlockSpec((tm, ne), row), pl.BlockSpec((1, ne), lambda i: (0, 0))],
        out_specs=[pl.BlockSpec((tm, LANES), row), pl.BlockSpec((tm, LANES), row)],
        out_shape=[jax.ShapeDtypeStruct((t, LANES), jnp.int32), jax.ShapeDtypeStruct((t, LANES), F32)],
        compiler_params=_cparams(("arbitrary",)),
        name="route",
    )(logits, bias.reshape(1, ne))


def _experts_kernel(blk_e_ref, nused_ref, x_ref, w_ref, wg_ref, wu_ref, wd_ref, y_ref):
    i = pl.program_id(0)

    @pl.when(i < nused_ref[0])
    def _():
        x = x_ref[...]
        hg = _dot(x, wg_ref[0].astype(BF16))
        hu = _dot(x, wu_ref[0].astype(BF16))
        hb = (hg * jax.nn.sigmoid(hg)) * hu
        y = _dot(hb.astype(BF16), wd_ref[0].astype(BF16))
        y_ref[...] = (y * w_ref[...]).astype(y_ref.dtype)

    @pl.when(i >= nused_ref[0])
    def _():
        y_ref[...] = jnp.zeros(y_ref.shape, y_ref.dtype)


def _experts(blk_e, nused, xs, row_w, w_gate, w_up, w_down):
    rows, d = xs.shape
    bm = MOE_BM
    nb = rows // bm
    de = w_gate.shape[2]
    xmap = lambda i, be, nu: (jnp.minimum(i, nu[0] - 1), 0)
    wmap = lambda i, be, nu: (be[i], 0, 0)
    grid_spec = pltpu.PrefetchScalarGridSpec(
        num_scalar_prefetch=2,
        grid=(nb,),
        in_specs=[pl.BlockSpec((bm, d), xmap), pl.BlockSpec((bm, 1), xmap),
                  pl.BlockSpec((1, d, de), wmap), pl.BlockSpec((1, d, de), wmap), pl.BlockSpec((1, de, d), wmap)],
        out_specs=pl.BlockSpec((bm, d), lambda i, be, nu: (i, 0)),
    )
    return pl.pallas_call(
        _experts_kernel,
        grid_spec=grid_spec,
        out_shape=jax.ShapeDtypeStruct((rows, d), BF16),
        compiler_params=_cparams(("arbitrary",)),
        name="experts",
    )(blk_e, nused, xs, row_w, w_gate, w_up, w_down)


def _dispatch_tables(eidx, n_tokens):
    bm = MOE_BM
    a = n_tokens * TOP_K
    nb = -(-(a + N_EXPERTS * (bm - 1)) // bm)
    flat_e = eidx.reshape(-1)
    order = jnp.argsort(flat_e, stable=True).astype(jnp.int32)
    e_sorted = flat_e[order]
    experts = jnp.arange(N_EXPERTS, dtype=jnp.int32)
    start = jnp.searchsorted(e_sorted, experts, side="left").astype(jnp.int32)
    counts = jnp.searchsorted(e_sorted, experts, side="right").astype(jnp.int32) - start
    padded = (counts + bm - 1) // bm * bm
    pad_end = jnp.cumsum(padded)
    pad_start = pad_end - padded
    nused = (pad_end[-1] // bm).astype(jnp.int32)
    blk_e = jnp.minimum(jnp.searchsorted(pad_end, jnp.arange(nb, dtype=jnp.int32) * bm, side="right"),
                        N_EXPERTS - 1).astype(jnp.int32)
    p = jnp.arange(nb * bm, dtype=jnp.int32)
    e_p = blk_e[p // bm]
    i_p = p - pad_start[e_p]
    valid = (i_p < counts[e_p]) & (p < pad_end[-1])
    src = jnp.where(valid, order[jnp.clip(start[e_p] + i_p, 0, a - 1)], 0)
    row_tok = jnp.where(valid, src // TOP_K, n_tokens)
    dest_sorted = pad_start[e_sorted] + jnp.arange(a, dtype=jnp.int32) - start[e_sorted]
    pos = jnp.zeros((a,), jnp.int32).at[order].set(dest_sorted)
    return blk_e, nused.reshape(1), row_tok, src, valid, pos


def _final_kernel(x1_ref, h2_ref, yr_ref, g2_ref, wsg_ref, wsu_ref, wsd_ref, fg_ref, o_ref):
    h = h2_ref[...]
    hg = _dot(h, wsg_ref[...])
    hu = _dot(h, wsu_ref[...])
    ys = _dot(((hg * jax.nn.sigmoid(hg)) * hu).astype(BF16), wsd_ref[...])
    x2 = x1_ref[...] + g2_ref[0] * (yr_ref[...].astype(F32) + ys)
    y = x2 * lax.rsqrt(jnp.mean(x2 * x2, axis=-1, keepdims=True) + RMS_EPS)
    o_ref[...] = y * fg_ref[...]


def _final(x1, h2, yr, g2, wsg, wsu, wsd, fg, seq):
    t, d = x1.shape
    tm = TM_ROWS
    per_b = seq // tm
    row = lambda i: (i, 0)
    full = lambda i: (0, 0)
    return pl.pallas_call(
        _final_kernel,
        grid=(t // tm,),
        in_specs=[pl.BlockSpec((tm, d), row), pl.BlockSpec((tm, d), row), pl.BlockSpec((tm, d), row),
                  pl.BlockSpec((1, 1, d), lambda i: (i // per_b, 0, 0)),
                  pl.BlockSpec(wsg.shape, full), pl.BlockSpec(wsu.shape, full), pl.BlockSpec(wsd.shape, full),
                  pl.BlockSpec((1, d), full)],
        out_specs=pl.BlockSpec((tm, d), row),
        out_shape=jax.ShapeDtypeStruct((t, d), F32),
        compiler_params=_cparams(("arbitrary",)),
        name="final",
    )(x1, h2, yr, g2, wsg, wsu, wsd, fg)


def _layer(x, c, w_ada, b_ada, norm1_g, w_in, cmp_pe_k, cmp_w1_k, cmp_b1_k, cmp_w2_k,
           cmp_pe_v, cmp_w1_v, cmp_b1_v, cmp_w2_v, w_up_a, w_up_b, w_out, norm2_g,
           w_router, router_bias, w_e_gate, w_e_up, w_e_down, w_s_gate, w_s_up, w_s_down):
    bsz, seq, d = x.shape
    t = bsz * seq
    x2 = x.reshape(t, d)
    slopes_a, slopes_b = _alibi_slopes()

    ada = _ada(c, w_ada, b_ada)
    sh1, sc1, g1, sh2, sc2, g2 = [a[:, None, :] for a in jnp.split(ada, 6, axis=-1)]

    qa, kva, ga, qkvb, gates = _inproj(x2, sc1, sh1, norm1_g.reshape(1, d), _pack_w_in(w_in), seq)

    hkv, grp, dh = NSA_KV_HEADS, NSA_GROUP, HEAD_DIM
    kv6 = kva.reshape(bsz, seq, 6, hkv, dh).transpose(2, 0, 3, 1, 4)
    n16 = seq // CMP_STRIDE
    cmp_in = kv6[0:2].reshape(2, bsz * hkv, n16, CMP_STRIDE * dh)
    pe = jnp.stack([cmp_pe_k, cmp_pe_v]).reshape(2, 1, CMP_LEN * dh)
    pe = jnp.broadcast_to(pe, (2, 8, CMP_LEN * dh))
    kvc = _compress(cmp_in, pe, jnp.stack([cmp_w1_k, cmp_w1_v]),
                    jnp.stack([cmp_b1_k, cmp_b1_v]).reshape(2, 1, CMP_HIDDEN),
                    jnp.stack([cmp_w2_k, cmp_w2_v]))
    kvc = kvc.reshape(2, bsz, hkv, n16, dh).astype(BF16)

    def with_features(base, feats):
        f = jnp.broadcast_to(feats, base.shape[:-1] + feats.shape[-1:])
        padw = LANES - base.shape[-1] - feats.shape[-1]
        return jnp.concatenate([base, f, jnp.zeros(base.shape[:-1] + (padw,), BF16)], axis=-1)

    cidx = np.arange(n16)
    kc_aug = with_features(kvc[0], _pos_features(64.0 * (cidx // 4), 16.0 * (cidx % 4) + (CMP_LEN - 1) / 2))
    pos = np.arange(seq)
    key_feats = _pos_features(64.0 * (pos // 64), pos % 64)
    ns = seq // SEL_BLOCK
    own_block = jnp.asarray((pos[:, None] // SEL_BLOCK == np.arange(ns)[None, :]) * BIG, BF16)
    ks_aug = jnp.concatenate([with_features(kv6[2], key_feats),
                              jnp.broadcast_to(own_block, (bsz, hkv, seq, ns))], axis=-1)
    kw_aug = with_features(kv6[4], key_feats)
    qa5 = qa.reshape(bsz, seq, hkv, grp, dh).transpose(0, 2, 3, 1, 4)
    qa_aug = with_features(qa5, _slope_features(slopes_a).reshape(1, hkv, grp, 1, 6))
    ga_h = ga[:, :NSA_GATES].reshape(bsz, seq, hkv, 3 * grp).transpose(0, 2, 1, 3)
    ga_h = jnp.pad(ga_h, ((0, 0), (0, 0), (0, 0), (0, LANES - 3 * grp)))
    o_a = _nsa(qa_aug, kc_aug, kvc[1], ks_aug, kv6[3], kw_aug, kv6[5], ga_h,
               _importance_matrix(n16, ns))

    hb = MOBA_HEADS
    qkv = qkvb.reshape(bsz, seq, 3, hb, dh).transpose(2, 0, 3, 1, 4)
    nbk = seq // MOBA_BLOCK
    qb_aug = with_features(qkv[0], _slope_features(slopes_b).reshape(1, hb, 1, 6))
    own_b = jnp.asarray((pos[:, None] // MOBA_BLOCK == np.arange(nbk)[None, :]) * BIG, BF16)
    kb_aug = with_features(qkv[1], key_feats)
    kb_aug = jnp.concatenate([kb_aug[..., :MOBA_SEL0], jnp.broadcast_to(own_b, (bsz, hb, seq, nbk))], axis=-1)
    km = _kmean(qkv[1]).astype(BF16)
    km_aug = jnp.pad(km, ((0, 0), (0, 0), (MOBA_SEL0, LANES - MOBA_SEL0 - nbk), (0, LANES - dh)))
    o_b = _moba(qb_aug, km_aug, kb_aug, qkv[2])
    o_b = o_b.transpose(0, 2, 1, 3).reshape(t, MOBA_W)

    x1, h2, logits = _merge(o_a.reshape(t, NSA_Q), o_b, gates, x2, g1, sc2, sh2, norm2_g.reshape(1, d),
                            w_up_a.astype(BF16), w_up_b.astype(BF16), w_out.astype(BF16), w_router, seq)

    eidx128, wts128 = _route(logits, router_bias)
    eidx = eidx128[:, :TOP_K]
    wts = wts128[:, :TOP_K]
    blk_e, nused, row_tok, src, valid, posn = _dispatch_tables(eidx, t)
    h2_pad = jnp.concatenate([h2, jnp.zeros((1, d), BF16)], axis=0)
    xs = jnp.take(h2_pad, row_tok, axis=0)
    row_w = jnp.where(valid, wts.reshape(-1)[src], 0.0).reshape(-1, 1)
    y_rows = _experts(blk_e, nused, xs, row_w, w_e_gate, w_e_up, w_e_down)
    yr = jnp.sum(jnp.take(y_rows, posn, axis=0).astype(F32).reshape(t, TOP_K, d), axis=1)

    return x1, h2, yr, g2


def kernel(x, c, w_ada, b_ada, norm1_g, w_in, cmp_pe_k, cmp_w1_k, cmp_b1_k, cmp_w2_k, cmp_pe_v, cmp_w1_v, cmp_b1_v, cmp_w2_v, w_up_a, w_up_b, w_out, norm2_g, w_router, router_bias, w_e_gate, w_e_up, w_e_down, w_s_gate, w_s_up, w_s_down, final_g):
    bsz, seq, d = x.shape
    depth = w_ada.shape[0]
    assert depth == 1, "the final-norm fusion below assumes a single layer"
    l = 0
    x1, h2, yr, g2 = _layer(x, c, w_ada[l], b_ada[l], norm1_g[l], w_in[l], cmp_pe_k[l], cmp_w1_k[l], cmp_b1_k[l],
                            cmp_w2_k[l], cmp_pe_v[l], cmp_w1_v[l], cmp_b1_v[l], cmp_w2_v[l], w_up_a[l], w_up_b[l],
                            w_out[l], norm2_g[l], w_router[l], router_bias[l], w_e_gate[l], w_e_up[l], w_e_down[l],
                            w_s_gate[l], w_s_up[l], w_s_down[l])
    out = _final(x1, h2, yr, g2, w_s_gate[l].astype(BF16), w_s_up[l].astype(BF16), w_s_down[l].astype(BF16),
                 final_g.reshape(1, d), seq)
    return out.reshape(bsz, seq, d)
```

```python
import functools

import numpy as np
import jax
import jax.numpy as jnp
from jax import lax
from jax.experimental import pallas as pl
from jax.experimental.pallas import tpu as pltpu

D_MODEL = 1024
HEAD_DIM = 64
NSA_HEADS = 8
NSA_KV_HEADS = 2
NSA_GROUP = NSA_HEADS // NSA_KV_HEADS
CMP_STRIDE = 16
CMP_LEN = 2 * CMP_STRIDE
CMP_HIDDEN = 256
SEL_BLOCK = 64
N_SEL = 16
WINDOW = 512
MOBA_HEADS = 8
MOBA_BLOCK = 256
MOBA_TOPK = 3
N_EXPERTS = 256
TOP_K = 8
N_GROUPS = 8
TOPK_GROUPS = 4
D_EXPERT = 256
D_SHARED = 256
ROUTE_SCALE = 2.5
RMS_EPS = 1e-6

NSA_Q = NSA_HEADS * HEAD_DIM
NSA_KV = NSA_KV_HEADS * HEAD_DIM
NSA_GATES = NSA_HEADS * 3
MOBA_W = MOBA_HEADS * HEAD_DIM

LANES = 128
VMEM_LIMIT = 52 * 1024 * 1024

SEG_QA = (0, NSA_Q)
SEG_KVA = (SEG_QA[1], SEG_QA[1] + 6 * NSA_KV)
SEG_GA = (SEG_KVA[1], SEG_KVA[1] + LANES)
SEG_QKVB = (SEG_GA[1], SEG_GA[1] + 3 * MOBA_W)
SEG_GATES = (SEG_QKVB[1], SEG_QKVB[1] + 2 * D_MODEL)
IN_COLS_PACKED = SEG_GATES[1]

BIG = float(2 ** 30)
NEG = -1e30
FEAT0 = HEAD_DIM
MOBA_SEL0 = 96

TM_ROWS = 256
NSA_TQ = 128
NSA_TK = 512
NSA_CHAINS = 2
MOBA_TQ = MOBA_BLOCK
MOBA_TK = 1024
MOBA_HP = 2
MOE_BM = 256
DISPATCH_TM = 128

F32 = jnp.float32
BF16 = jnp.bfloat16


def _cparams(sem):
    return pltpu.CompilerParams(dimension_semantics=sem, vmem_limit_bytes=VMEM_LIMIT)


def _dot_t(a, b):
    return lax.dot_general(a, b, (((1,), (1,)), ((), ())), preferred_element_type=F32)


def _dot(a, b):
    return jnp.dot(a, b, preferred_element_type=F32)


def _split3(a):
    hi = a.astype(BF16)
    r = a - hi.astype(F32)
    mid = r.astype(BF16)
    lo = (r - mid.astype(F32)).astype(BF16)
    return hi, mid, lo


def _ada_kernel(c_ref, w_ref, b_ref, o_ref):
    c = c_ref[...]
    a = c * jax.nn.sigmoid(c)
    a1, a2, a3 = _split3(a)
    w1, w2, w3 = _split3(w_ref[...])
    acc = _dot(a1, w1) + (_dot(a1, w2) + _dot(a2, w1)) + (_dot(a1, w3) + _dot(a2, w2) + _dot(a3, w1))
    o_ref[...] = acc + b_ref[...]


def _ada(c, w, b):
    bsz, d = c.shape
    n = w.shape[1]
    cp = jnp.zeros((8, d), F32).at[:bsz].set(c)
    out = pl.pallas_call(
        _ada_kernel,
        grid=(n // d,),
        in_specs=[pl.BlockSpec((8, d), lambda j: (0, 0)),
                  pl.BlockSpec((d, d), lambda j: (0, j)),
                  pl.BlockSpec((1, d), lambda j: (0, j))],
        out_specs=pl.BlockSpec((8, d), lambda j: (0, j)),
        out_shape=jax.ShapeDtypeStruct((8, n), F32),
        compiler_params=_cparams(("arbitrary",)),
        name="ada",
    )(cp, w, b.reshape(1, n))
    return out[:bsz]


def _modulated_rmsnorm(x, g, sc, sh):
    y = x * lax.rsqrt(jnp.mean(x * x, axis=-1, keepdims=True) + RMS_EPS)
    return (y * g) * (1.0 + sc) + sh


def _inproj_kernel(x_ref, sc_ref, sh_ref, g_ref, w_ref, qa_ref, kva_ref, ga_ref, qkvb_ref, gates_ref):
    h = _modulated_rmsnorm(x_ref[...], g_ref[...], sc_ref[0], sh_ref[0]).astype(BF16)
    scale = HEAD_DIM ** -0.5
    qa_ref[...] = (_dot(h, w_ref[:, SEG_QA[0]:SEG_QA[1]]) * scale).astype(BF16)
    kva_ref[...] = _dot(h, w_ref[:, SEG_KVA[0]:SEG_KVA[1]]).astype(BF16)
    ga_ref[...] = jax.nn.sigmoid(_dot(h, w_ref[:, SEG_GA[0]:SEG_GA[1]]))
    qb0 = SEG_QKVB[0]
    qkvb_ref[:, :MOBA_W] = (_dot(h, w_ref[:, qb0:qb0 + MOBA_W]) * scale).astype(BF16)
    qkvb_ref[:, MOBA_W:] = _dot(h, w_ref[:, qb0 + MOBA_W:SEG_QKVB[1]]).astype(BF16)
    gates_ref[...] = jax.nn.sigmoid(_dot(h, w_ref[:, SEG_GATES[0]:SEG_GATES[1]])).astype(BF16)


def _inproj(x2, sc, sh, g, w_packed, seq):
    t, d = x2.shape
    tm = TM_ROWS
    per_b = seq // tm
    row = lambda i: (i, 0)
    bat = lambda i: (i // per_b, 0, 0)
    widths = (NSA_Q, 6 * NSA_KV, LANES, 3 * MOBA_W, 2 * D_MODEL)
    dts = (BF16, BF16, F32, BF16, BF16)
    return pl.pallas_call(
        _inproj_kernel,
        grid=(t // tm,),
        in_specs=[pl.BlockSpec((tm, d), row),
                  pl.BlockSpec((1, 1, d), bat),
                  pl.BlockSpec((1, 1, d), bat),
                  pl.BlockSpec((1, d), lambda i: (0, 0)),
                  pl.BlockSpec((d, IN_COLS_PACKED), lambda i: (0, 0))],
        out_specs=[pl.BlockSpec((tm, w), row) for w in widths],
        out_shape=[jax.ShapeDtypeStruct((t, w), dt) for w, dt in zip(widths, dts)],
        compiler_params=_cparams(("arbitrary",)),
        name="inproj",
    )(x2, sc, sh, g, w_packed)


def _pack_w_in(w_in):
    sizes = [NSA_Q, 6 * NSA_KV, NSA_GATES, 3 * MOBA_W, D_MODEL, D_MODEL]
    offs = np.cumsum([0] + sizes)
    parts = [w_in[:, offs[i]:offs[i + 1]] for i in range(6)]
    parts[2] = jnp.pad(parts[2], ((0, 0), (0, LANES - NSA_GATES)))
    return jnp.concatenate(parts, axis=1).astype(BF16)


def _compress_kernel(x_ref, pe_ref, w1_ref, b1_ref, w2_ref, o_ref):
    x = x_ref[0, 0]
    half = CMP_STRIDE * HEAD_DIM
    w1 = w1_ref[0].astype(BF16)
    first = _dot(x, w1[:half])
    second = _dot(x, w1[half:])
    const = _dot(pe_ref[0].astype(BF16), w1)[0:1] + b1_ref[0]
    n = x.shape[0]
    hidden = first + pltpu.roll(second, n - 1, 0) + const
    o_ref[0, 0] = _dot(jax.nn.gelu(hidden).astype(BF16), w2_ref[0].astype(BF16))


def _compress(x, pe, w1, b1, w2):
    two, bh, n, kd = x.shape
    return pl.pallas_call(
        _compress_kernel,
        grid=(two, bh),
        in_specs=[pl.BlockSpec((1, 1, n, kd), lambda a, b: (a, b, 0, 0)),
                  pl.BlockSpec((1,) + pe.shape[1:], lambda a, b: (a, 0, 0)),
                  pl.BlockSpec((1,) + w1.shape[1:], lambda a, b: (a, 0, 0)),
                  pl.BlockSpec((1,) + b1.shape[1:], lambda a, b: (a, 0, 0)),
                  pl.BlockSpec((1,) + w2.shape[1:], lambda a, b: (a, 0, 0))],
        out_specs=pl.BlockSpec((1, 1, n, HEAD_DIM), lambda a, b: (a, b, 0, 0)),
        out_shape=jax.ShapeDtypeStruct((two, bh, n, HEAD_DIM), F32),
        compiler_params=_cparams(("arbitrary", "arbitrary")),
        name="compress",
    )(x, pe, w1, b1, w2)


def _alibi_slopes():
    n = NSA_HEADS + MOBA_HEADS
    s = (2.0 ** (-8.0 * np.arange(1, n + 1) / n)).astype(np.float32)
    return s[0::2], s[1::2]


def _slope_features(slopes):
    s = jnp.asarray(slopes, F32)
    h1, h2, h3 = _split3(s)
    return jnp.stack([h1, h2, h3, h1, h2, h3], axis=-1)


def _pos_features(pos_hi, pos_lo):
    hi = np.asarray(pos_hi, np.float32)
    lo = np.asarray(pos_lo, np.float32)
    return jnp.asarray(np.stack([hi, hi, hi, lo, lo, lo], axis=-1), BF16)


def _with_ones(v):
    ones = jnp.ones(v.shape[:-1] + (1,), BF16)
    zeros = jnp.zeros(v.shape[:-1] + (LANES - v.shape[-1] - 1,), BF16)
    return jnp.concatenate([v.astype(BF16), ones, zeros], axis=-1)


def _flash_step(s, v_ones, m_sc, acc_sc):
    m_prev = m_sc[...]
    m_new = jnp.maximum(m_prev, jnp.max(s, axis=-1, keepdims=True))
    p = jnp.exp(s - m_new)
    acc_sc[...] = jnp.exp(m_prev - m_new) * acc_sc[...] + _dot(p.astype(BF16), v_ones)
    m_sc[...] = m_new


def _flash_init(m_sc, acc_sc):
    m_sc[...] = jnp.full(m_sc.shape, NEG, F32)
    acc_sc[...] = jnp.zeros(acc_sc.shape, F32)


def _flash_out(acc):
    return acc[:, :HEAD_DIM] / jnp.maximum(acc[:, HEAD_DIM:HEAD_DIM + 1], 1e-30)


def _nsa_kernel(q_ref, kc_ref, vc_ref, ks_ref, vs_ref, kw_ref, vw_ref, g_ref, impm_ref, o_ref,
                m_sc, acc_sc, *, tq, tk):
    t0 = pl.program_id(2) * tq
    rows = NSA_GROUP * tq
    half = rows // NSA_CHAINS
    q0 = q_ref[0, 0].reshape(rows, LANES)
    t_q = t0 + lax.broadcasted_iota(jnp.int32, (tq, 1), 0)
    t_r = jnp.concatenate([t_q] * NSA_GROUP, axis=0)

    nc = kc_ref.shape[2]
    sc = _dot_t(q0, kc_ref[0, 0])
    c_end = lax.broadcasted_iota(jnp.int32, (1, nc), 1) * CMP_STRIDE + (CMP_LEN - 1)
    sc = jnp.where(c_end <= t_r, sc, -jnp.inf)
    m = jnp.max(sc, axis=-1, keepdims=True)
    m = jnp.where(m == -jnp.inf, 0.0, m)
    e = jnp.exp(sc - m)
    p_c = e / jnp.maximum(jnp.sum(e, axis=-1, keepdims=True), 1e-30)
    o_c = _dot(p_c.astype(BF16), vc_ref[0, 0])

    p_grp = p_c[0:tq]
    for g in range(1, NSA_GROUP):
        p_grp = p_grp + p_c[g * tq:(g + 1) * tq]
    p1, p2, p3 = _split3(p_grp)
    impm = impm_ref[...]
    imp = _dot_t(impm, p1) + _dot_t(impm, p2) + _dot_t(impm, p3)
    ns = imp.shape[0]
    jb = lax.broadcasted_iota(jnp.int32, (ns, 1), 0).astype(F32)
    cur = jnp.right_shift(t0 + lax.broadcasted_iota(jnp.int32, (1, tq), 1), SEL_BLOCK.bit_length() - 1).astype(F32)
    forced = (jb == 0.0) | (jb == cur) | (jb == cur - 1.0)
    work = jnp.where(forced, jnp.inf, imp)
    work = jnp.where(jb > cur, -jnp.inf, work)
    selm1 = jnp.full((ns, tq), -1.0, F32)
    for _ in range(N_SEL):
        mx = jnp.max(work, axis=0, keepdims=True)
        idx = jnp.min(jnp.where(work == mx, jb, float(ns)), axis=0, keepdims=True)
        pick = jb == idx
        selm1 = jnp.where(pick, 0.0, selm1)
        work = jnp.where(pick, -jnp.inf, work)
    selm1 = selm1.T.astype(BF16)
    sel_rows = jnp.concatenate([selm1] * (NSA_GROUP // NSA_CHAINS), axis=0)
    q_full = [jnp.concatenate([q0[c * half:(c + 1) * half], sel_rows], axis=1) for c in range(NSA_CHAINS)]
    t_h = t_r[:half]

    for c in range(NSA_CHAINS):
        _flash_init(m_sc.at[c], acc_sc.at[c])

    def tile(k0, masked):
        k = ks_ref[0, 0, pl.ds(k0, tk), :]
        v = vs_ref[0, 0, pl.ds(k0, tk), :]
        for c in range(NSA_CHAINS):
            s = _dot_t(q_full[c], k)
            if masked:
                s = jnp.where(k0 + lax.broadcasted_iota(jnp.int32, (1, tk), 1) <= t_h, s, NEG)
            _flash_step(s, v, m_sc.at[c], acc_sc.at[c])

    kt_diag = lax.shift_right_logical(t0, tk.bit_length() - 1)

    def body(kt, carry):
        tile(pl.multiple_of(kt * tk, tk), False)
        return carry

    lax.fori_loop(0, kt_diag, body, 0)
    tile(pl.multiple_of(kt_diag * tk, tk), True)
    o_s = jnp.concatenate([_flash_out(acc_sc[c]) for c in range(NSA_CHAINS)], axis=0)

    band = WINDOW + tq
    w0 = pl.multiple_of(jnp.maximum(t0 - WINDOW, 0), tq)
    kw = kw_ref[0, 0, pl.ds(w0, band), :]
    vw = vw_ref[0, 0, pl.ds(w0, band), :]
    dist = (t_h - w0) - lax.broadcasted_iota(jnp.int32, (1, band), 1)
    in_window = lax.bitcast_convert_type(dist, jnp.uint32) < jnp.uint32(WINDOW)
    o_w = []
    for c in range(NSA_CHAINS):
        sw = jnp.where(in_window, _dot_t(q0[c * half:(c + 1) * half], kw), NEG)
        ew = jnp.exp(sw - jnp.max(sw, axis=-1, keepdims=True))
        o_w.append(_flash_out(_dot(ew.astype(BF16), vw)))
    o_w = jnp.concatenate(o_w, axis=0)

    gates = g_ref[0, 0]
    outs = []
    for g in range(NSA_GROUP):
        r = slice(g * tq, (g + 1) * tq)
        outs.append(gates[:, 3 * g:3 * g + 1] * o_c[r] + gates[:, 3 * g + 1:3 * g + 2] * o_s[r]
                    + gates[:, 3 * g + 2:3 * g + 3] * o_w[r])
    o_ref[0] = jnp.concatenate(outs, axis=1).astype(BF16)


def _nsa(q_aug, kc_aug, vc, ks_aug, vs, kw_aug, vw, gates, impm):
    bsz, hkv, grp, seq, _ = q_aug.shape
    tq, tk = NSA_TQ, NSA_TK
    nc = kc_aug.shape[2]
    half = grp * tq // NSA_CHAINS
    kv_map = lambda b, h, i: (b, h, 0, 0)
    return pl.pallas_call(
        functools.partial(_nsa_kernel, tq=tq, tk=tk),
        grid=(bsz, hkv, seq // tq),
        in_specs=[pl.BlockSpec((1, 1, grp, tq, LANES), lambda b, h, i: (b, h, 0, i, 0)),
                  pl.BlockSpec((1, 1, nc, LANES), kv_map),
                  pl.BlockSpec((1, 1, nc, HEAD_DIM), kv_map),
                  pl.BlockSpec((1, 1, seq, 2 * LANES), kv_map),
                  pl.BlockSpec((1, 1, seq, LANES), kv_map),
                  pl.BlockSpec((1, 1, seq, LANES), kv_map),
                  pl.BlockSpec((1, 1, seq, LANES), kv_map),
                  pl.BlockSpec((1, 1, tq, LANES), lambda b, h, i: (b, h, i, 0)),
                  pl.BlockSpec(impm.shape, lambda b, h, i: (0, 0))],
        out_specs=pl.BlockSpec((1, tq, grp * HEAD_DIM), lambda b, h, i: (b, i, h)),
        out_shape=jax.ShapeDtypeStruct((bsz, seq, hkv * grp * HEAD_DIM), BF16),
        scratch_shapes=[pltpu.VMEM((NSA_CHAINS, half, 1), F32), pltpu.VMEM((NSA_CHAINS, half, LANES), F32)],
        compiler_params=_cparams(("arbitrary", "arbitrary", "arbitrary")),
        name="nsa",
    )(q_aug, kc_aug, vc, ks_aug, vs, kw_aug, vw, gates, impm)


def _importance_matrix(nc, ns):
    ratio = SEL_BLOCK // CMP_STRIDE
    span = CMP_LEN // CMP_STRIDE
    m = np.zeros((ns, nc), np.float32)
    for j in range(ns):
        for r in range(ratio):
            for s in range(span):
                c = ratio * j + r - s
                if 0 <= c < nc:
                    m[j, c] += 1.0
    return jnp.asarray(m, BF16)


def _kmean_kernel(k_ref, o_ref):
    k = k_ref[0, 0].astype(F32)
    nb = k.shape[0] // MOBA_BLOCK
    o_ref[0, 0] = jnp.sum(k.reshape(nb, MOBA_BLOCK, HEAD_DIM), axis=1) / MOBA_BLOCK


def _kmean(k):
    bsz, h, seq, dh = k.shape
    nb = seq // MOBA_BLOCK
    return pl.pallas_call(
        _kmean_kernel,
        grid=(bsz, h),
        in_specs=[pl.BlockSpec((1, 1, seq, dh), lambda b, j: (b, j, 0, 0))],
        out_specs=pl.BlockSpec((1, 1, nb, dh), lambda b, j: (b, j, 0, 0)),
        out_shape=jax.ShapeDtypeStruct((bsz, h, nb, dh), F32),
        compiler_params=_cparams(("arbitrary", "arbitrary")),
        name="kmean",
    )(k)


def _moba_kernel(q_ref, km_ref, k_ref, v_ref, o_ref, m_sc, acc_sc, *, tq, tk, hp):
    cur = pl.program_id(2)
    t0 = cur * tq
    lane = lax.broadcasted_iota(jnp.int32, (1, LANES), 1)
    lane_f = lane.astype(F32)
    jb = lane - MOBA_SEL0
    past = lax.bitcast_convert_type(jb, jnp.uint32) < cur.astype(jnp.uint32)
    q_full = []
    for h in range(hp):
        q0 = q_ref[0, h]
        gate = _dot_t(q0, km_ref[0, h])
        work = jnp.where(past, gate, -jnp.inf)
        picked = jnp.zeros((tq, LANES), F32)
        for _ in range(MOBA_TOPK):
            mx = jnp.max(work, axis=-1, keepdims=True)
            idx = jnp.min(jnp.where(work == mx, lane_f, float(LANES)), axis=-1, keepdims=True)
            pick = lane_f == idx
            picked = jnp.where(pick, 1.0, picked)
            work = jnp.where(pick, -jnp.inf, work)
        keep = jnp.where(past, picked, 0.0) + jnp.where(jb == cur, 1.0, 0.0)
        q_full.append(jnp.where(lane >= MOBA_SEL0, (keep - 1.0).astype(BF16), q0))
        _flash_init(m_sc.at[h], acc_sc.at[h])
    t_col = t0 + lax.broadcasted_iota(jnp.int32, (tq, 1), 0)

    def tile(k0, masked):
        for h in range(hp):
            s = _dot_t(q_full[h], k_ref[0, h, pl.ds(k0, tk), :])
            if masked:
                s = jnp.where(k0 + lax.broadcasted_iota(jnp.int32, (1, tk), 1) <= t_col, s, NEG)
            _flash_step(s, v_ref[0, h, pl.ds(k0, tk), :], m_sc.at[h], acc_sc.at[h])

    def body(kt, carry):
        tile(pl.multiple_of(kt * tk, tk), False)
        return carry

    n_full = lax.shift_right_logical(t0, tk.bit_length() - 1)
    lax.fori_loop(0, n_full, body, 0)
    tile(pl.multiple_of(n_full * tk, tk), True)
    o_ref[0] = jnp.concatenate([_flash_out(acc_sc[h]) for h in range(hp)], axis=1).astype(BF16)


def _moba(q_aug, km_aug, k_aug, v_ones):
    bsz, h, seq, _ = q_aug.shape
    tq, tk, hp = MOBA_TQ, MOBA_TK, MOBA_HP
    kv_map = lambda b, j, i: (b, j, 0, 0)
    return pl.pallas_call(
        functools.partial(_moba_kernel, tq=tq, tk=tk, hp=hp),
        grid=(bsz, h // hp, seq // tq),
        in_specs=[pl.BlockSpec((1, hp, tq, LANES), lambda b, j, i: (b, j, i, 0)),
                  pl.BlockSpec((1, hp, LANES, LANES), kv_map),
                  pl.BlockSpec((1, hp, seq, LANES), kv_map),
                  pl.BlockSpec((1, hp, seq, LANES), kv_map)],
        out_specs=pl.BlockSpec((1, tq, hp * HEAD_DIM), lambda b, j, i: (b, i, j)),
        out_shape=jax.ShapeDtypeStruct((bsz, seq, h * HEAD_DIM), BF16),
        scratch_shapes=[pltpu.VMEM((hp, tq, 1), F32), pltpu.VMEM((hp, tq, LANES), F32)],
        compiler_params=_cparams(("arbitrary", "arbitrary", "arbitrary")),
        name="moba",
    )(q_aug, km_aug, k_aug, v_ones)


def _merge_kernel(oa_ref, ob_ref, gates_ref, x_ref, g1_ref, sc_ref, sh_ref, n2_ref, wua_ref, wub_ref, wout_ref,
                  wr_ref, x1_ref, h2_ref, logit_ref):
    ua = _dot(oa_ref[...], wua_ref[...])
    ub = _dot(ob_ref[...], wub_ref[...])
    merged = gates_ref[:, :D_MODEL].astype(F32) * ua + gates_ref[:, D_MODEL:].astype(F32) * ub
    mix = _dot(merged.astype(BF16), wout_ref[...])
    x1 = x_ref[...] + g1_ref[0] * mix
    x1_ref[...] = x1
    h2 = _modulated_rmsnorm(x1, n2_ref[...], sc_ref[0], sh_ref[0])
    h2_ref[...] = h2
    h2b = h2.astype(BF16)
    h2lo = (h2 - h2b.astype(F32)).astype(BF16)
    w1, w2, _ = _split3(wr_ref[...])
    logit_ref[...] = _dot(h2b, w1) + (_dot(h2b, w2) + _dot(h2lo, w1))


def _merge(oa, ob, gates, x2, g1, sc2, sh2, n2, wua, wub, wout, wr, seq):
    t, d = x2.shape
    tm = TM_ROWS
    per_b = seq // tm
    row = lambda i: (i, 0)
    bat = lambda i: (i // per_b, 0, 0)
    full = lambda i: (0, 0)
    return pl.pallas_call(
        _merge_kernel,
        grid=(t // tm,),
        in_specs=[pl.BlockSpec((tm, oa.shape[1]), row), pl.BlockSpec((tm, ob.shape[1]), row),
                  pl.BlockSpec((tm, 2 * d), row), pl.BlockSpec((tm, d), row),
                  pl.BlockSpec((1, 1, d), bat), pl.BlockSpec((1, 1, d), bat), pl.BlockSpec((1, 1, d), bat),
                  pl.BlockSpec((1, d), full),
                  pl.BlockSpec(wua.shape, full), pl.BlockSpec(wub.shape, full), pl.BlockSpec(wout.shape, full),
                  pl.BlockSpec(wr.shape, full)],
        out_specs=[pl.BlockSpec((tm, d), row), pl.BlockSpec((tm, d), row), pl.BlockSpec((tm, N_EXPERTS), row)],
        out_shape=[jax.ShapeDtypeStruct((t, d), F32), jax.ShapeDtypeStruct((t, d), F32),
                   jax.ShapeDtypeStruct((t, N_EXPERTS), F32)],
        compiler_params=_cparams(("arbitrary",)),
        name="merge",
    )(oa, ob, gates, x2, g1, sc2, sh2, n2, wua, wub, wout, wr)


def _route_kernel(logit_ref, bias_ref, eidx_ref, wts_ref, cnt_ref):
    @pl.when(pl.program_id(0) == 0)
    def _():
        cnt_ref[...] = jnp.zeros(cnt_ref.shape, F32)

    s = jax.nn.sigmoid(logit_ref[...])
    sb = s + bias_ref[...]
    tm, ne = s.shape
    per_g = ne // N_GROUPS
    lane_i = lax.broadcasted_iota(jnp.int32, (1, ne), 1)
    grp = jnp.right_shift(lane_i, per_g.bit_length() - 1)
    lane = lane_i.astype(F32)

    def lowest_argmax(v):
        mx = jnp.max(v, axis=-1, keepdims=True)
        return mx, jnp.min(jnp.where(v == mx, lane, float(ne)), axis=-1, keepdims=True)

    gscore = []
    for g in range(N_GROUPS):
        vg = jnp.where(grp == g, sb, -jnp.inf)
        m1, i1 = lowest_argmax(vg)
        m2 = jnp.max(jnp.where(lane == i1, -jnp.inf, vg), axis=-1, keepdims=True)
        gscore.append(m1 + m2)
    emask = jnp.zeros((tm, ne), F32)
    for g in range(N_GROUPS):
        rank = jnp.zeros((tm, 1), F32)
        for h in range(N_GROUPS):
            if h == g:
                continue
            ahead = (gscore[h] >= gscore[g]) if h < g else (gscore[h] > gscore[g])
            rank = rank + jnp.where(ahead, 1.0, 0.0)
        emask = jnp.where(grp == g, jnp.where(rank < TOPK_GROUPS, 1.0, 0.0), emask)
    work = jnp.where(emask > 0.5, sb, -jnp.inf)
    out_lane = lax.broadcasted_iota(jnp.int32, (1, LANES), 1)
    eidx = jnp.zeros((tm, LANES), F32)
    wts = jnp.zeros((tm, LANES), F32)
    chosen = jnp.zeros((tm, ne), F32)
    for k in range(TOP_K):
        _, idx = lowest_argmax(work)
        pick = lane == idx
        wk = jnp.sum(jnp.where(pick, s, 0.0), axis=-1, keepdims=True)
        work = jnp.where(pick, -jnp.inf, work)
        chosen = jnp.where(pick, 1.0, chosen)
        eidx = jnp.where(out_lane == k, idx, eidx)
        wts = jnp.where(out_lane == k, wk, wts)
    eidx_ref[...] = eidx.astype(jnp.int32)
    wts_ref[...] = wts / jnp.sum(wts, axis=-1, keepdims=True) * ROUTE_SCALE
    cnt_ref[...] = cnt_ref[...] + jnp.sum(chosen, axis=0, keepdims=True)


def _route(logits, bias):
    t, ne = logits.shape
    tm = TM_ROWS
    row = lambda i: (i, 0)
    return pl.pallas_call(
        _route_kernel,
        grid=(t // tm,),
        in_specs=[pl.BlockSpec((tm, ne), row), pl.BlockSpec((1, ne), lambda i: (0, 0))],
        out_specs=[pl.BlockSpec((tm, LANES), row), pl.BlockSpec((tm, LANES), row),
                   pl.BlockSpec((8, ne), lambda i: (0, 0))],
        out_shape=[jax.ShapeDtypeStruct((t, LANES), jnp.int32), jax.ShapeDtypeStruct((t, LANES), F32),
                   jax.ShapeDtypeStruct((8, ne), F32)],
        compiler_params=_cparams(("arbitrary",)),
        name="route",
    )(logits, bias.reshape(1, ne))


def _rank_kernel(eidx_ref, pstart_ref, ltri_ref, dest_ref, run_sc):
    @pl.when(pl.program_id(0) == 0)
    def _():
        run_sc[...] = jnp.zeros(run_sc.shape, F32)

    e = eidx_ref[...]
    tm = e.shape[0]
    ne = pstart_ref.shape[1]
    lane = lax.broadcasted_iota(jnp.int32, (1, ne), 1)
    onehot = [lane == e[:, k:k + 1] for k in range(TOP_K)]
    cnt = jnp.zeros((tm, ne), F32)
    for oh in onehot:
        cnt = cnt + jnp.where(oh, 1.0, 0.0)
    before = _dot(ltri_ref[...], cnt.astype(BF16)) + (run_sc[...] + pstart_ref[...])
    out_lane = lax.broadcasted_iota(jnp.int32, (1, LANES), 1)
    dest = jnp.zeros((tm, LANES), F32)
    for k, oh in enumerate(onehot):
        dk = jnp.sum(jnp.where(oh, before, 0.0), axis=-1, keepdims=True)
        dest = jnp.where(out_lane == k, dk, dest)
    dest_ref[...] = dest.astype(jnp.int32)
    run_sc[...] = run_sc[...] + jnp.sum(cnt, axis=0, keepdims=True)


def _rank(eidx128, pad_start):
    t = eidx128.shape[0]
    tm = TM_ROWS
    ne = pad_start.shape[0]
    ltri = jnp.asarray(np.tril(np.ones((tm, tm), np.float32), -1), BF16)
    return pl.pallas_call(
        _rank_kernel,
        grid=(t // tm,),
        in_specs=[pl.BlockSpec((tm, LANES), lambda i: (i, 0)), pl.BlockSpec((1, ne), lambda i: (0, 0)),
                  pl.BlockSpec((tm, tm), lambda i: (0, 0))],
        out_specs=pl.BlockSpec((tm, LANES), lambda i: (i, 0)),
        out_shape=jax.ShapeDtypeStruct((t, LANES), jnp.int32),
        scratch_shapes=[pltpu.VMEM((1, ne), F32)],
        compiler_params=_cparams(("arbitrary",)),
        name="rank",
    )(eidx128, pad_start.astype(F32).reshape(1, ne), ltri)


def _row_copy(src_ref, src_row, dst_ref, dst_row, sem):
    return pltpu.make_async_copy(src_ref.at[pl.ds(src_row, 1), :], dst_ref.at[pl.ds(dst_row, 1), :], sem)


def _dispatch_kernel(dest_ref, h_ref, xs_init_ref, xs_ref, sem):
    del xs_init_ref
    n = dest_ref.shape[0]
    log2k = TOP_K.bit_length() - 1

    def copy(j):
        return _row_copy(h_ref, lax.shift_right_logical(j, log2k), xs_ref, dest_ref[j], sem)

    def issue(j, carry):
        copy(j).start()
        return carry

    def drain(j, carry):
        copy(j).wait()
        return carry

    lax.fori_loop(0, n, issue, 0, unroll=8)
    lax.fori_loop(0, n, drain, 0, unroll=8)


def _dispatch(dest_flat, h2, rows):
    t, d = h2.shape
    tm = DISPATCH_TM
    return pl.pallas_call(
        _dispatch_kernel,
        grid=(t // tm,),
        in_specs=[pl.BlockSpec((tm * TOP_K,), lambda i: (i,), memory_space=pltpu.SMEM),
                  pl.BlockSpec((tm, d), lambda i: (i, 0)),
                  pl.BlockSpec(memory_space=pl.ANY)],
        out_specs=pl.BlockSpec(memory_space=pl.ANY),
        out_shape=jax.ShapeDtypeStruct((rows, d), F32),
        scratch_shapes=[pltpu.SemaphoreType.DMA(())],
        input_output_aliases={2: 0},
        compiler_params=_cparams(("arbitrary",)),
        name="dispatch",
    )(dest_flat, h2, jnp.zeros((rows, d), F32))


def _experts_kernel(blk_e_ref, nused_ref, x_ref, wg_ref, wu_ref, wd_ref, y_ref):
    i = pl.program_id(0)

    @pl.when(i < nused_ref[0])
    def _():
        x = x_ref[...].astype(BF16)
        hg = _dot(x, wg_ref[0].astype(BF16))
        hu = _dot(x, wu_ref[0].astype(BF16))
        hb = (hg * jax.nn.sigmoid(hg)) * hu
        y_ref[...] = _dot(hb.astype(BF16), wd_ref[0].astype(BF16))

    @pl.when(i >= nused_ref[0])
    def _():
        y_ref[...] = jnp.zeros(y_ref.shape, y_ref.dtype)


def _experts(blk_e, nused, xs, w_gate, w_up, w_down):
    rows, d = xs.shape
    bm = MOE_BM
    nb = rows // bm
    de = w_gate.shape[2]
    xmap = lambda i, be, nu: (jnp.minimum(i, nu[0] - 1), 0)
    wmap = lambda i, be, nu: (be[i], 0, 0)
    grid_spec = pltpu.PrefetchScalarGridSpec(
        num_scalar_prefetch=2,
        grid=(nb,),
        in_specs=[pl.BlockSpec((bm, d), xmap),
                  pl.BlockSpec((1, d, de), wmap), pl.BlockSpec((1, d, de), wmap), pl.BlockSpec((1, de, d), wmap)],
        out_specs=pl.BlockSpec((bm, d), lambda i, be, nu: (i, 0)),
    )
    return pl.pallas_call(
        _experts_kernel,
        grid_spec=grid_spec,
        out_shape=jax.ShapeDtypeStruct((rows, d), F32),
        compiler_params=_cparams(("arbitrary",)),
        name="experts",
    )(blk_e, nused, xs, w_gate, w_up, w_down)


def _block_tables(counts, n_tokens):
    bm = MOE_BM
    nb = -(-(n_tokens * TOP_K + N_EXPERTS * (bm - 1)) // bm)
    padded = (counts + bm - 1) // bm * bm
    pad_end = jnp.cumsum(padded)
    pad_start = pad_end - padded
    nused = (pad_end[-1] // bm).astype(jnp.int32).reshape(1)
    first_row = jnp.arange(nb, dtype=jnp.int32) * bm
    blk_e = jnp.minimum(jnp.sum((pad_end[None, :] <= first_row[:, None]).astype(jnp.int32), axis=1), N_EXPERTS - 1)
    return blk_e.astype(jnp.int32), nused, pad_start, nb * bm


def _final_kernel(dest_ref, x1_ref, h2_ref, wts_ref, y_hbm, g2_ref, wsg_ref, wsu_ref, wsd_ref, fg_ref, o_ref,
                  ybuf, sem):
    n = dest_ref.shape[0]
    log2k = TOP_K.bit_length() - 1

    def copy(j):
        k = jnp.bitwise_and(j, TOP_K - 1)
        return _row_copy(y_hbm, dest_ref[j], ybuf.at[k], lax.shift_right_logical(j, log2k), sem)

    def issue(j, carry):
        copy(j).start()
        return carry

    def drain(j, carry):
        copy(j).wait()
        return carry

    lax.fori_loop(0, n, issue, 0, unroll=8)
    h = h2_ref[...].astype(BF16)
    hg = _dot(h, wsg_ref[...])
    hu = _dot(h, wsu_ref[...])
    y = _dot(((hg * jax.nn.sigmoid(hg)) * hu).astype(BF16), wsd_ref[...])
    lax.fori_loop(0, n, drain, 0, unroll=8)
    w = wts_ref[...]
    for k in range(TOP_K):
        y = y + w[:, k:k + 1] * ybuf[k]
    x2 = x1_ref[...] + g2_ref[0] * y
    o_ref[...] = (x2 * lax.rsqrt(jnp.mean(x2 * x2, axis=-1, keepdims=True) + RMS_EPS)) * fg_ref[...]


def _final(dest_flat, x1, h2, wts128, y_rows, g2, wsg, wsu, wsd, fg, seq):
    t, d = x1.shape
    tm = DISPATCH_TM
    per_b = seq // tm
    row = lambda i: (i, 0)
    full = lambda i: (0, 0)
    return pl.pallas_call(
        _final_kernel,
        grid=(t // tm,),
        in_specs=[pl.BlockSpec((tm * TOP_K,), lambda i: (i,), memory_space=pltpu.SMEM),
                  pl.BlockSpec((tm, d), row), pl.BlockSpec((tm, d), row), pl.BlockSpec((tm, LANES), row),
                  pl.BlockSpec(memory_space=pl.ANY),
                  pl.BlockSpec((1, 1, d), lambda i: (i // per_b, 0, 0)),
                  pl.BlockSpec(wsg.shape, full), pl.BlockSpec(wsu.shape, full), pl.BlockSpec(wsd.shape, full),
                  pl.BlockSpec((1, d), full)],
        out_specs=pl.BlockSpec((tm, d), row),
        out_shape=jax.ShapeDtypeStruct((t, d), F32),
        scratch_shapes=[pltpu.VMEM((TOP_K, tm, d), F32), pltpu.SemaphoreType.DMA(())],
        compiler_params=_cparams(("arbitrary",)),
        name="final",
    )(dest_flat, x1, h2, wts128, y_rows, g2, wsg, wsu, wsd, fg)


def _with_features(base, feats):
    f = jnp.broadcast_to(feats, base.shape[:-1] + feats.shape[-1:])
    padw = LANES - base.shape[-1] - feats.shape[-1]
    return jnp.concatenate([base, f, jnp.zeros(base.shape[:-1] + (padw,), BF16)], axis=-1)


def _key_features(seq):
    pos = np.arange(seq)
    return _pos_features(64.0 * (pos // 64), pos % 64)


def _own_block(seq, block):
    pos = np.arange(seq)
    return jnp.asarray((pos[:, None] // block == np.arange(seq // block)[None, :]) * BIG, BF16)


def _mixer_a(qa, kva, ga, bsz, seq, cmp_pe_k, cmp_w1_k, cmp_b1_k, cmp_w2_k, cmp_pe_v, cmp_w1_v, cmp_b1_v, cmp_w2_v):
    hkv, grp, dh = NSA_KV_HEADS, NSA_GROUP, HEAD_DIM
    slopes_a, _ = _alibi_slopes()
    kv6 = kva.reshape(bsz, seq, 6, hkv, dh).transpose(2, 0, 3, 1, 4)
    n16 = seq // CMP_STRIDE
    cmp_in = kv6[0:2].reshape(2, bsz * hkv, n16, CMP_STRIDE * dh)
    pe = jnp.stack([cmp_pe_k, cmp_pe_v]).reshape(2, 1, CMP_LEN * dh)
    pe = jnp.broadcast_to(pe, (2, 8, CMP_LEN * dh))
    kvc = _compress(cmp_in, pe, jnp.stack([cmp_w1_k, cmp_w1_v]),
                    jnp.stack([cmp_b1_k, cmp_b1_v]).reshape(2, 1, CMP_HIDDEN),
                    jnp.stack([cmp_w2_k, cmp_w2_v]))
    kvc = kvc.reshape(2, bsz, hkv, n16, dh).astype(BF16)
    cidx = np.arange(n16)
    kc_aug = _with_features(kvc[0], _pos_features(64.0 * (cidx // 4), 16.0 * (cidx % 4) + (CMP_LEN - 1) / 2))
    key_feats = _key_features(seq)
    ns = seq // SEL_BLOCK
    ks_aug = jnp.concatenate([_with_features(kv6[2], key_feats),
                              jnp.broadcast_to(_own_block(seq, SEL_BLOCK), (bsz, hkv, seq, ns))], axis=-1)
    kw_aug = _with_features(kv6[4], key_feats)
    qa5 = qa.reshape(bsz, seq, hkv, grp, dh).transpose(0, 2, 3, 1, 4)
    qa_aug = _with_features(qa5, _slope_features(slopes_a).reshape(1, hkv, grp, 1, 6))
    ga_h = ga[:, :NSA_GATES].reshape(bsz, seq, hkv, 3 * grp).transpose(0, 2, 1, 3)
    ga_h = jnp.pad(ga_h, ((0, 0), (0, 0), (0, 0), (0, LANES - 3 * grp)))
    return _nsa(qa_aug, kc_aug, kvc[1], ks_aug, _with_ones(kv6[3]), kw_aug, _with_ones(kv6[5]), ga_h,
                _importance_matrix(n16, ns))


def _mixer_b(qkvb, bsz, seq):
    hb, dh = MOBA_HEADS, HEAD_DIM
    _, slopes_b = _alibi_slopes()
    qkv = qkvb.reshape(bsz, seq, 3, hb, dh).transpose(2, 0, 3, 1, 4)
    nbk = seq // MOBA_BLOCK
    qb_aug = _with_features(qkv[0], _slope_features(slopes_b).reshape(1, hb, 1, 6))
    kb_aug = _with_features(qkv[1], _key_features(seq))
    kb_aug = jnp.concatenate([kb_aug[..., :MOBA_SEL0],
                              jnp.broadcast_to(_own_block(seq, MOBA_BLOCK), (bsz, hb, seq, nbk))], axis=-1)
    km = _kmean(qkv[1]).astype(BF16)
    km_aug = jnp.pad(km, ((0, 0), (0, 0), (MOBA_SEL0, LANES - MOBA_SEL0 - nbk), (0, LANES - dh)))
    return _moba(qb_aug, km_aug, kb_aug, _with_ones(qkv[2]))


def _layer(x, c, w_ada, b_ada, norm1_g, w_in, cmp_pe_k, cmp_w1_k, cmp_b1_k, cmp_w2_k,
           cmp_pe_v, cmp_w1_v, cmp_b1_v, cmp_w2_v, w_up_a, w_up_b, w_out, norm2_g,
           w_router, router_bias, w_e_gate, w_e_up, w_e_down, w_s_gate, w_s_up, w_s_down):
    bsz, seq, d = x.shape
    t = bsz * seq
    x2 = x.reshape(t, d)
    slopes_a, slopes_b = _alibi_slopes()

    ada = _ada(c, w_ada, b_ada)
    sh1, sc1, g1, sh2, sc2, g2 = [a[:, None, :] for a in jnp.split(ada, 6, axis=-1)]

    qa, kva, ga, qkvb, gates = _inproj(x2, sc1, sh1, norm1_g.reshape(1, d), _pack_w_in(w_in), seq)
    o_a = _mixer_a(qa, kva, ga, bsz, seq, cmp_pe_k, cmp_w1_k, cmp_b1_k, cmp_w2_k,
                   cmp_pe_v, cmp_w1_v, cmp_b1_v, cmp_w2_v)
    o_b = _mixer_b(qkvb, bsz, seq)
    x1, h2, logits = _merge(o_a.reshape(t, NSA_Q), o_b.reshape(t, MOBA_W), gates, x2, g1, sc2, sh2,
                            norm2_g.reshape(1, d), w_up_a.astype(BF16), w_up_b.astype(BF16), w_out.astype(BF16),
                            w_router, seq)

    eidx128, wts128, counts8 = _route(logits, router_bias)
    blk_e, nused, pad_start, rows = _block_tables(counts8[0].astype(jnp.int32), t)
    dest_flat = _rank(eidx128, pad_start)[:, :TOP_K].reshape(-1)
    xs = _dispatch(dest_flat, h2, rows)
    y_rows = _experts(blk_e, nused, xs, w_e_gate, w_e_up, w_e_down)
    return dest_flat, x1, h2, wts128, y_rows, g2


def kernel(x, c, w_ada, b_ada, norm1_g, w_in, cmp_pe_k, cmp_w1_k, cmp_b1_k, cmp_w2_k, cmp_pe_v, cmp_w1_v, cmp_b1_v, cmp_w2_v, w_up_a, w_up_b, w_out, norm2_g, w_router, router_bias, w_e_gate, w_e_up, w_e_down, w_s_gate, w_s_up, w_s_down, final_g):
    bsz, seq, d = x.shape
    depth = w_ada.shape[0]
    assert depth == 1, "the final-norm fusion below assumes a single layer"
    l = 0
    moe = _layer(x, c, w_ada[l], b_ada[l], norm1_g[l], w_in[l], cmp_pe_k[l], cmp_w1_k[l], cmp_b1_k[l],
                 cmp_w2_k[l], cmp_pe_v[l], cmp_w1_v[l], cmp_b1_v[l], cmp_w2_v[l], w_up_a[l], w_up_b[l],
                 w_out[l], norm2_g[l], w_router[l], router_bias[l], w_e_gate[l], w_e_up[l], w_e_down[l],
                 w_s_gate[l], w_s_up[l], w_s_down[l])
    out = _final(*moe, w_s_gate[l].astype(BF16), w_s_up[l].astype(BF16), w_s_down[l].astype(BF16),
                 final_g.reshape(1, d), seq)
    return out.reshape(bsz, seq, d)
```

```python
import functools

import numpy as np
import jax
import jax.numpy as jnp
from jax import lax
from jax.experimental import pallas as pl
from jax.experimental.pallas import tpu as pltpu

D_MODEL = 1024
HEAD_DIM = 64
NSA_HEADS = 8
NSA_KV_HEADS = 2
NSA_GROUP = NSA_HEADS // NSA_KV_HEADS
CMP_STRIDE = 16
CMP_LEN = 2 * CMP_STRIDE
CMP_HIDDEN = 256
SEL_BLOCK = 64
N_SEL = 16
WINDOW = 512
MOBA_HEADS = 8
MOBA_BLOCK = 256
MOBA_TOPK = 3
N_EXPERTS = 256
TOP_K = 8
N_GROUPS = 8
TOPK_GROUPS = 4
D_EXPERT = 256
D_SHARED = 256
ROUTE_SCALE = 2.5
RMS_EPS = 1e-6

NSA_Q = NSA_HEADS * HEAD_DIM
NSA_KV = NSA_KV_HEADS * HEAD_DIM
NSA_GATES = NSA_HEADS * 3
MOBA_W = MOBA_HEADS * HEAD_DIM

LANES = 128
VMEM_LIMIT = 52 * 1024 * 1024

SEG_QA = (0, NSA_Q)
SEG_KVA = (SEG_QA[1], SEG_QA[1] + 6 * NSA_KV)
SEG_GA = (SEG_KVA[1], SEG_KVA[1] + LANES)
SEG_QKVB = (SEG_GA[1], SEG_GA[1] + 3 * MOBA_W)
SEG_GATES = (SEG_QKVB[1], SEG_QKVB[1] + 2 * D_MODEL)
IN_COLS_PACKED = SEG_GATES[1]

BIG = float(2 ** 30)
NEG = -1e30
FEAT0 = HEAD_DIM
MOBA_SEL0 = 96

TM_ROWS = 256
NSA_TQ = 256
NSA_TK = 2048
NSA_CHAINS = 2
MOBA_TQ = MOBA_BLOCK
MOBA_TK = 2048
MOBA_HP = 4
MOE_BM = 256
DISPATCH_TM = 128

F32 = jnp.float32
BF16 = jnp.bfloat16


def _cparams(sem):
    return pltpu.CompilerParams(dimension_semantics=sem, vmem_limit_bytes=VMEM_LIMIT)


def _dot_t(a, b):
    return lax.dot_general(a, b, (((1,), (1,)), ((), ())), preferred_element_type=F32)


def _dot(a, b):
    return jnp.dot(a, b, preferred_element_type=F32)


def _split3(a):
    hi = a.astype(BF16)
    r = a - hi.astype(F32)
    mid = r.astype(BF16)
    lo = (r - mid.astype(F32)).astype(BF16)
    return hi, mid, lo


def _ada_kernel(c_ref, w_ref, b_ref, o_ref):
    c = c_ref[...]
    a = c * jax.nn.sigmoid(c)
    a1, a2, a3 = _split3(a)
    w1, w2, w3 = _split3(w_ref[...])
    acc = _dot(a1, w1) + (_dot(a1, w2) + _dot(a2, w1)) + (_dot(a1, w3) + _dot(a2, w2) + _dot(a3, w1))
    o_ref[...] = acc + b_ref[...]


def _ada(c, w, b):
    bsz, d = c.shape
    n = w.shape[1]
    cp = jnp.zeros((8, d), F32).at[:bsz].set(c)
    out = pl.pallas_call(
        _ada_kernel,
        grid=(n // d,),
        in_specs=[pl.BlockSpec((8, d), lambda j: (0, 0)),
                  pl.BlockSpec((d, d), lambda j: (0, j)),
                  pl.BlockSpec((1, d), lambda j: (0, j))],
        out_specs=pl.BlockSpec((8, d), lambda j: (0, j)),
        out_shape=jax.ShapeDtypeStruct((8, n), F32),
        compiler_params=_cparams(("arbitrary",)),
        name="ada",
    )(cp, w, b.reshape(1, n))
    return out[:bsz]


def _modulated_rmsnorm(x, g, sc, sh):
    y = x * lax.rsqrt(jnp.mean(x * x, axis=-1, keepdims=True) + RMS_EPS)
    return (y * g) * (1.0 + sc) + sh


def _inproj_kernel(x_ref, sc_ref, sh_ref, g_ref, w_ref, qa_ref, kva_ref, ga_ref, qkvb_ref, gates_ref):
    h = _modulated_rmsnorm(x_ref[...], g_ref[...], sc_ref[0], sh_ref[0]).astype(BF16)
    scale = HEAD_DIM ** -0.5
    qa_ref[...] = (_dot(h, w_ref[:, SEG_QA[0]:SEG_QA[1]]) * scale).astype(BF16)
    kva_ref[...] = _dot(h, w_ref[:, SEG_KVA[0]:SEG_KVA[1]]).astype(BF16)
    ga_ref[...] = jax.nn.sigmoid(_dot(h, w_ref[:, SEG_GA[0]:SEG_GA[1]]))
    qb0 = SEG_QKVB[0]
    qkvb_ref[:, :MOBA_W] = (_dot(h, w_ref[:, qb0:qb0 + MOBA_W]) * scale).astype(BF16)
    qkvb_ref[:, MOBA_W:] = _dot(h, w_ref[:, qb0 + MOBA_W:SEG_QKVB[1]]).astype(BF16)
    gates_ref[...] = jax.nn.sigmoid(_dot(h, w_ref[:, SEG_GATES[0]:SEG_GATES[1]])).astype(BF16)


def _inproj(x2, sc, sh, g, w_packed, seq):
    t, d = x2.shape
    tm = TM_ROWS
    per_b = seq // tm
    row = lambda i: (i, 0)
    bat = lambda i: (i // per_b, 0, 0)
    widths = (NSA_Q, 6 * NSA_KV, LANES, 3 * MOBA_W, 2 * D_MODEL)
    dts = (BF16, BF16, F32, BF16, BF16)
    return pl.pallas_call(
        _inproj_kernel,
        grid=(t // tm,),
        in_specs=[pl.BlockSpec((tm, d), row),
                  pl.BlockSpec((1, 1, d), bat),
                  pl.BlockSpec((1, 1, d), bat),
                  pl.BlockSpec((1, d), lambda i: (0, 0)),
                  pl.BlockSpec((d, IN_COLS_PACKED), lambda i: (0, 0))],
        out_specs=[pl.BlockSpec((tm, w), row) for w in widths],
        out_shape=[jax.ShapeDtypeStruct((t, w), dt) for w, dt in zip(widths, dts)],
        compiler_params=_cparams(("arbitrary",)),
        name="inproj",
    )(x2, sc, sh, g, w_packed)


def _pack_w_in(w_in):
    sizes = [NSA_Q, 6 * NSA_KV, NSA_GATES, 3 * MOBA_W, D_MODEL, D_MODEL]
    offs = np.cumsum([0] + sizes)
    parts = [w_in[:, offs[i]:offs[i + 1]] for i in range(6)]
    parts[2] = jnp.pad(parts[2], ((0, 0), (0, LANES - NSA_GATES)))
    return jnp.concatenate(parts, axis=1).astype(BF16)


def _compress_kernel(x_ref, pe_ref, w1_ref, b1_ref, w2_ref, o_ref):
    x = x_ref[0, 0]
    half = CMP_STRIDE * HEAD_DIM
    w1 = w1_ref[0].astype(BF16)
    first = _dot(x, w1[:half])
    second = _dot(x, w1[half:])
    const = _dot(pe_ref[0].astype(BF16), w1)[0:1] + b1_ref[0]
    n = x.shape[0]
    hidden = first + pltpu.roll(second, n - 1, 0) + const
    o_ref[0, 0] = _dot(jax.nn.gelu(hidden).astype(BF16), w2_ref[0].astype(BF16))


def _compress(x, pe, w1, b1, w2):
    two, bh, n, kd = x.shape
    return pl.pallas_call(
        _compress_kernel,
        grid=(two, bh),
        in_specs=[pl.BlockSpec((1, 1, n, kd), lambda a, b: (a, b, 0, 0)),
                  pl.BlockSpec((1,) + pe.shape[1:], lambda a, b: (a, 0, 0)),
                  pl.BlockSpec((1,) + w1.shape[1:], lambda a, b: (a, 0, 0)),
                  pl.BlockSpec((1,) + b1.shape[1:], lambda a, b: (a, 0, 0)),
                  pl.BlockSpec((1,) + w2.shape[1:], lambda a, b: (a, 0, 0))],
        out_specs=pl.BlockSpec((1, 1, n, HEAD_DIM), lambda a, b: (a, b, 0, 0)),
        out_shape=jax.ShapeDtypeStruct((two, bh, n, HEAD_DIM), F32),
        compiler_params=_cparams(("arbitrary", "arbitrary")),
        name="compress",
    )(x, pe, w1, b1, w2)


def _alibi_slopes():
    n = NSA_HEADS + MOBA_HEADS
    s = (2.0 ** (-8.0 * np.arange(1, n + 1) / n)).astype(np.float32)
    return s[0::2], s[1::2]


def _slope_features(slopes):
    s = jnp.asarray(slopes, F32)
    h1, h2, h3 = _split3(s)
    return jnp.stack([h1, h2, h3, h1, h2, h3], axis=-1)


def _pos_features(pos_hi, pos_lo):
    hi = np.asarray(pos_hi, np.float32)
    lo = np.asarray(pos_lo, np.float32)
    return jnp.asarray(np.stack([hi, hi, hi, lo, lo, lo], axis=-1), BF16)


def _with_ones(v):
    ones = jnp.ones(v.shape[:-1] + (1,), BF16)
    zeros = jnp.zeros(v.shape[:-1] + (LANES - v.shape[-1] - 1,), BF16)
    return jnp.concatenate([v.astype(BF16), ones, zeros], axis=-1)


def _flash_step(s, v_ones, m_sc, acc_sc):
    m_prev = m_sc[...]
    m_new = jnp.maximum(m_prev, jnp.max(s, axis=-1, keepdims=True))
    p = jnp.exp(s - m_new)
    acc_sc[...] = jnp.exp(m_prev - m_new) * acc_sc[...] + _dot(p.astype(BF16), v_ones)
    m_sc[...] = m_new


def _flash_init(m_sc, acc_sc):
    m_sc[...] = jnp.full(m_sc.shape, NEG, F32)
    acc_sc[...] = jnp.zeros(acc_sc.shape, F32)


def _flash_out(acc):
    return acc[:, :HEAD_DIM] / jnp.maximum(acc[:, HEAD_DIM:HEAD_DIM + 1], 1e-30)


def _nsa_kernel(q_ref, kc_ref, vc_ref, ks_ref, vs_ref, kw_ref, vw_ref, g_ref, impm_ref, o_ref,
                m_sc, acc_sc, *, tq, tk):
    t0 = pl.program_id(2) * tq
    rows = NSA_GROUP * tq
    half = rows // NSA_CHAINS
    q0 = q_ref[0, 0].reshape(rows, LANES)
    t_q = t0 + lax.broadcasted_iota(jnp.int32, (tq, 1), 0)
    t_r = jnp.concatenate([t_q] * NSA_GROUP, axis=0)

    nc = kc_ref.shape[2]
    sc = _dot_t(q0, kc_ref[0, 0])
    c_end = lax.broadcasted_iota(jnp.int32, (1, nc), 1) * CMP_STRIDE + (CMP_LEN - 1)
    sc = jnp.where(c_end <= t_r, sc, -jnp.inf)
    m = jnp.max(sc, axis=-1, keepdims=True)
    m = jnp.where(m == -jnp.inf, 0.0, m)
    e = jnp.exp(sc - m)
    p_c = e / jnp.maximum(jnp.sum(e, axis=-1, keepdims=True), 1e-30)
    o_c = _dot(p_c.astype(BF16), vc_ref[0, 0])

    p_grp = p_c[0:tq]
    for g in range(1, NSA_GROUP):
        p_grp = p_grp + p_c[g * tq:(g + 1) * tq]
    p1, p2, p3 = _split3(p_grp)
    impm = impm_ref[...]
    imp = _dot_t(impm, p1) + _dot_t(impm, p2) + _dot_t(impm, p3)
    ns = imp.shape[0]
    jb = lax.broadcasted_iota(jnp.int32, (ns, 1), 0).astype(F32)
    cur = jnp.right_shift(t0 + lax.broadcasted_iota(jnp.int32, (1, tq), 1), SEL_BLOCK.bit_length() - 1).astype(F32)
    forced = (jb == 0.0) | (jb == cur) | (jb == cur - 1.0)
    work = jnp.where(forced, jnp.inf, imp)
    work = jnp.where(jb > cur, -jnp.inf, work)
    selm1 = jnp.full((ns, tq), -1.0, F32)
    for _ in range(N_SEL):
        mx = jnp.max(work, axis=0, keepdims=True)
        idx = jnp.min(jnp.where(work == mx, jb, float(ns)), axis=0, keepdims=True)
        pick = jb == idx
        selm1 = jnp.where(pick, 0.0, selm1)
        work = jnp.where(pick, -jnp.inf, work)
    selm1 = selm1.T.astype(BF16)
    def chain_rows(a):
        if half >= tq:
            return [jnp.concatenate([a] * (half // tq), axis=0)] * NSA_CHAINS
        return [a[(c * half) % tq:(c * half) % tq + half] for c in range(NSA_CHAINS)]

    sel_rows = chain_rows(selm1)
    t_h = chain_rows(t_q)
    q_full = [jnp.concatenate([q0[c * half:(c + 1) * half], sel_rows[c]], axis=1) for c in range(NSA_CHAINS)]

    for c in range(NSA_CHAINS):
        _flash_init(m_sc.at[c], acc_sc.at[c])

    def tile(k0, masked):
        k = ks_ref[0, 0, pl.ds(k0, tk), :]
        v = vs_ref[0, 0, pl.ds(k0, tk), :]
        for c in range(NSA_CHAINS):
            s = _dot_t(q_full[c], k)
            if masked:
                s = jnp.where(k0 + lax.broadcasted_iota(jnp.int32, (1, tk), 1) <= t_h[c], s, NEG)
            _flash_step(s, v, m_sc.at[c], acc_sc.at[c])

    kt_diag = lax.shift_right_logical(t0, tk.bit_length() - 1)

    def body(kt, carry):
        tile(pl.multiple_of(kt * tk, tk), False)
        return carry

    lax.fori_loop(0, kt_diag, body, 0)
    tile(pl.multiple_of(kt_diag * tk, tk), True)
    o_s = jnp.concatenate([_flash_out(acc_sc[c]) for c in range(NSA_CHAINS)], axis=0)

    band = WINDOW + tq
    w0 = pl.multiple_of(jnp.maximum(t0 - WINDOW, 0), tq)
    kw = kw_ref[0, 0, pl.ds(w0, band), :]
    vw = vw_ref[0, 0, pl.ds(w0, band), :]
    o_w = []
    for c in range(NSA_CHAINS):
        dist = (t_h[c] - w0) - lax.broadcasted_iota(jnp.int32, (1, band), 1)
        in_window = lax.bitcast_convert_type(dist, jnp.uint32) < jnp.uint32(WINDOW)
        sw = jnp.where(in_window, _dot_t(q0[c * half:(c + 1) * half], kw), NEG)
        ew = jnp.exp(sw - jnp.max(sw, axis=-1, keepdims=True))
        o_w.append(_flash_out(_dot(ew.astype(BF16), vw)))
    o_w = jnp.concatenate(o_w, axis=0)

    gates = g_ref[0, 0]
    outs = []
    for g in range(NSA_GROUP):
        r = slice(g * tq, (g + 1) * tq)
        outs.append(gates[:, 3 * g:3 * g + 1] * o_c[r] + gates[:, 3 * g + 1:3 * g + 2] * o_s[r]
                    + gates[:, 3 * g + 2:3 * g + 3] * o_w[r])
    o_ref[0] = jnp.concatenate(outs, axis=1).astype(BF16)


def _nsa(q_aug, kc_aug, vc, ks_aug, vs, kw_aug, vw, gates, impm):
    bsz, hkv, grp, seq, _ = q_aug.shape
    tq, tk = NSA_TQ, NSA_TK
    nc = kc_aug.shape[2]
    half = grp * tq // NSA_CHAINS
    kv_map = lambda b, h, i: (b, h, 0, 0)
    return pl.pallas_call(
        functools.partial(_nsa_kernel, tq=tq, tk=tk),
        grid=(bsz, hkv, seq // tq),
        in_specs=[pl.BlockSpec((1, 1, grp, tq, LANES), lambda b, h, i: (b, h, 0, i, 0)),
                  pl.BlockSpec((1, 1, nc, LANES), kv_map),
                  pl.BlockSpec((1, 1, nc, HEAD_DIM), kv_map),
                  pl.BlockSpec((1, 1, seq, 2 * LANES), kv_map),
                  pl.BlockSpec((1, 1, seq, LANES), kv_map),
                  pl.BlockSpec((1, 1, seq, LANES), kv_map),
                  pl.BlockSpec((1, 1, seq, LANES), kv_map),
                  pl.BlockSpec((1, 1, tq, LANES), lambda b, h, i: (b, h, i, 0)),
                  pl.BlockSpec(impm.shape, lambda b, h, i: (0, 0))],
        out_specs=pl.BlockSpec((1, tq, grp * HEAD_DIM), lambda b, h, i: (b, i, h)),
        out_shape=jax.ShapeDtypeStruct((bsz, seq, hkv * grp * HEAD_DIM), BF16),
        scratch_shapes=[pltpu.VMEM((NSA_CHAINS, half, 1), F32), pltpu.VMEM((NSA_CHAINS, half, LANES), F32)],
        compiler_params=_cparams(("arbitrary", "arbitrary", "arbitrary")),
        name="nsa",
    )(q_aug, kc_aug, vc, ks_aug, vs, kw_aug, vw, gates, impm)


def _importance_matrix(nc, ns):
    ratio = SEL_BLOCK // CMP_STRIDE
    span = CMP_LEN // CMP_STRIDE
    m = np.zeros((ns, nc), np.float32)
    for j in range(ns):
        for r in range(ratio):
            for s in range(span):
                c = ratio * j + r - s
                if 0 <= c < nc:
                    m[j, c] += 1.0
    return jnp.asarray(m, BF16)


def _kmean_kernel(k_ref, o_ref):
    k = k_ref[0, 0].astype(F32)
    nb = k.shape[0] // MOBA_BLOCK
    o_ref[0, 0] = jnp.sum(k.reshape(nb, MOBA_BLOCK, HEAD_DIM), axis=1) / MOBA_BLOCK


def _kmean(k):
    bsz, h, seq, dh = k.shape
    nb = seq // MOBA_BLOCK
    return pl.pallas_call(
        _kmean_kernel,
        grid=(bsz, h),
        in_specs=[pl.BlockSpec((1, 1, seq, dh), lambda b, j: (b, j, 0, 0))],
        out_specs=pl.BlockSpec((1, 1, nb, dh), lambda b, j: (b, j, 0, 0)),
        out_shape=jax.ShapeDtypeStruct((bsz, h, nb, dh), F32),
        compiler_params=_cparams(("arbitrary", "arbitrary")),
        name="kmean",
    )(k)


def _moba_kernel(q_ref, km_ref, k_ref, v_ref, o_ref, m_sc, acc_sc, *, tq, tk, hp):
    cur = pl.program_id(2)
    t0 = cur * tq
    lane = lax.broadcasted_iota(jnp.int32, (1, LANES), 1)
    lane_f = lane.astype(F32)
    jb = lane - MOBA_SEL0
    past = lax.bitcast_convert_type(jb, jnp.uint32) < cur.astype(jnp.uint32)
    q_full = []
    for h in range(hp):
        q0 = q_ref[0, h]
        gate = _dot_t(q0, km_ref[0, h])
        work = jnp.where(past, gate, -jnp.inf)
        picked = jnp.zeros((tq, LANES), F32)
        for _ in range(MOBA_TOPK):
            mx = jnp.max(work, axis=-1, keepdims=True)
            idx = jnp.min(jnp.where(work == mx, lane_f, float(LANES)), axis=-1, keepdims=True)
            pick = lane_f == idx
            picked = jnp.where(pick, 1.0, picked)
            work = jnp.where(pick, -jnp.inf, work)
        keep = jnp.where(past, picked, 0.0) + jnp.where(jb == cur, 1.0, 0.0)
        q_full.append(jnp.where(lane >= MOBA_SEL0, (keep - 1.0).astype(BF16), q0))
        _flash_init(m_sc.at[h], acc_sc.at[h])
    t_col = t0 + lax.broadcasted_iota(jnp.int32, (tq, 1), 0)

    def tile(k0, masked):
        for h in range(hp):
            s = _dot_t(q_full[h], k_ref[0, h, pl.ds(k0, tk), :])
            if masked:
                s = jnp.where(k0 + lax.broadcasted_iota(jnp.int32, (1, tk), 1) <= t_col, s, NEG)
            _flash_step(s, v_ref[0, h, pl.ds(k0, tk), :], m_sc.at[h], acc_sc.at[h])

    def body(kt, carry):
        tile(pl.multiple_of(kt * tk, tk), False)
        return carry

    n_full = lax.shift_right_logical(t0, tk.bit_length() - 1)
    lax.fori_loop(0, n_full, body, 0)
    tile(pl.multiple_of(n_full * tk, tk), True)
    o_ref[0] = jnp.concatenate([_flash_out(acc_sc[h]) for h in range(hp)], axis=1).astype(BF16)


def _moba(q_aug, km_aug, k_aug, v_ones):
    bsz, h, seq, _ = q_aug.shape
    tq, tk, hp = MOBA_TQ, MOBA_TK, MOBA_HP
    kv_map = lambda b, j, i: (b, j, 0, 0)
    return pl.pallas_call(
        functools.partial(_moba_kernel, tq=tq, tk=tk, hp=hp),
        grid=(bsz, h // hp, seq // tq),
        in_specs=[pl.BlockSpec((1, hp, tq, LANES), lambda b, j, i: (b, j, i, 0)),
                  pl.BlockSpec((1, hp, LANES, LANES), kv_map),
                  pl.BlockSpec((1, hp, seq, LANES), kv_map),
                  pl.BlockSpec((1, hp, seq, LANES), kv_map)],
        out_specs=pl.BlockSpec((1, tq, hp * HEAD_DIM), lambda b, j, i: (b, i, j)),
        out_shape=jax.ShapeDtypeStruct((bsz, seq, h * HEAD_DIM), BF16),
        scratch_shapes=[pltpu.VMEM((hp, tq, 1), F32), pltpu.VMEM((hp, tq, LANES), F32)],
        compiler_params=_cparams(("arbitrary", "arbitrary", "arbitrary")),
        name="moba",
    )(q_aug, km_aug, k_aug, v_ones)


def _merge_kernel(oa_ref, ob_ref, gates_ref, x_ref, g1_ref, sc_ref, sh_ref, n2_ref, wua_ref, wub_ref, wout_ref,
                  wr_ref, x1_ref, h2_ref, logit_ref):
    ua = _dot(oa_ref[...], wua_ref[...])
    ub = _dot(ob_ref[...], wub_ref[...])
    merged = gates_ref[:, :D_MODEL].astype(F32) * ua + gates_ref[:, D_MODEL:].astype(F32) * ub
    mix = _dot(merged.astype(BF16), wout_ref[...])
    x1 = x_ref[...] + g1_ref[0] * mix
    x1_ref[...] = x1
    h2 = _modulated_rmsnorm(x1, n2_ref[...], sc_ref[0], sh_ref[0])
    h2_ref[...] = h2
    h2b = h2.astype(BF16)
    h2lo = (h2 - h2b.astype(F32)).astype(BF16)
    w1, w2, _ = _split3(wr_ref[...])
    logit_ref[...] = _dot(h2b, w1) + (_dot(h2b, w2) + _dot(h2lo, w1))


def _merge(oa, ob, gates, x2, g1, sc2, sh2, n2, wua, wub, wout, wr, seq):
    t, d = x2.shape
    tm = TM_ROWS
    per_b = seq // tm
    row = lambda i: (i, 0)
    bat = lambda i: (i // per_b, 0, 0)
    full = lambda i: (0, 0)
    return pl.pallas_call(
        _merge_kernel,
        grid=(t // tm,),
        in_specs=[pl.BlockSpec((tm, oa.shape[1]), row), pl.BlockSpec((tm, ob.shape[1]), row),
                  pl.BlockSpec((tm, 2 * d), row), pl.BlockSpec((tm, d), row),
                  pl.BlockSpec((1, 1, d), bat), pl.BlockSpec((1, 1, d), bat), pl.BlockSpec((1, 1, d), bat),
                  pl.BlockSpec((1, d), full),
                  pl.BlockSpec(wua.shape, full), pl.BlockSpec(wub.shape, full), pl.BlockSpec(wout.shape, full),
                  pl.BlockSpec(wr.shape, full)],
        out_specs=[pl.BlockSpec((tm, d), row), pl.BlockSpec((tm, d), row), pl.BlockSpec((tm, N_EXPERTS), row)],
        out_shape=[jax.ShapeDtypeStruct((t, d), F32), jax.ShapeDtypeStruct((t, d), F32),
                   jax.ShapeDtypeStruct((t, N_EXPERTS), F32)],
        compiler_params=_cparams(("arbitrary",)),
        name="merge",
    )(oa, ob, gates, x2, g1, sc2, sh2, n2, wua, wub, wout, wr)


def _route_kernel(logit_ref, bias_ref, eidx_ref, wts_ref, cnt_ref):
    @pl.when(pl.program_id(0) == 0)
    def _():
        cnt_ref[...] = jnp.zeros(cnt_ref.shape, F32)

    s = jax.nn.sigmoid(logit_ref[...])
    sb = s + bias_ref[...]
    tm, ne = s.shape
    per_g = ne // N_GROUPS
    lane_i = lax.broadcasted_iota(jnp.int32, (1, ne), 1)
    grp = jnp.right_shift(lane_i, per_g.bit_length() - 1)
    lane = lane_i.astype(F32)

    def lowest_argmax(v):
        mx = jnp.max(v, axis=-1, keepdims=True)
        return mx, jnp.min(jnp.where(v == mx, lane, float(ne)), axis=-1, keepdims=True)

    gscore = []
    for g in range(N_GROUPS):
        vg = jnp.where(grp == g, sb, -jnp.inf)
        m1, i1 = lowest_argmax(vg)
        m2 = jnp.max(jnp.where(lane == i1, -jnp.inf, vg), axis=-1, keepdims=True)
        gscore.append(m1 + m2)
    emask = jnp.zeros((tm, ne), F32)
    for g in range(N_GROUPS):
        rank = jnp.zeros((tm, 1), F32)
        for h in range(N_GROUPS):
            if h == g:
                continue
            ahead = (gscore[h] >= gscore[g]) if h < g else (gscore[h] > gscore[g])
            rank = rank + jnp.where(ahead, 1.0, 0.0)
        emask = jnp.where(grp == g, jnp.where(rank < TOPK_GROUPS, 1.0, 0.0), emask)
    work = jnp.where(emask > 0.5, sb, -jnp.inf)
    out_lane = lax.broadcasted_iota(jnp.int32, (1, LANES), 1)
    eidx = jnp.zeros((tm, LANES), F32)
    wts = jnp.zeros((tm, LANES), F32)
    chosen = jnp.zeros((tm, ne), F32)
    for k in range(TOP_K):
        _, idx = lowest_argmax(work)
        pick = lane == idx
        wk = jnp.sum(jnp.where(pick, s, 0.0), axis=-1, keepdims=True)
        work = jnp.where(pick, -jnp.inf, work)
        chosen = jnp.where(pick, 1.0, chosen)
        eidx = jnp.where(out_lane == k, idx, eidx)
        wts = jnp.where(out_lane == k, wk, wts)
    eidx_ref[...] = eidx.astype(jnp.int32)
    wts_ref[...] = wts / jnp.sum(wts, axis=-1, keepdims=True) * ROUTE_SCALE
    cnt_ref[...] = cnt_ref[...] + jnp.sum(chosen, axis=0, keepdims=True)


def _route(logits, bias):
    t, ne = logits.shape
    tm = TM_ROWS
    row = lambda i: (i, 0)
    return pl.pallas_call(
        _route_kernel,
        grid=(t // tm,),
        in_specs=[pl.BlockSpec((tm, ne), row), pl.BlockSpec((1, ne), lambda i: (0, 0))],
        out_specs=[pl.BlockSpec((tm, LANES), row), pl.BlockSpec((tm, LANES), row),
                   pl.BlockSpec((8, ne), lambda i: (0, 0))],
        out_shape=[jax.ShapeDtypeStruct((t, LANES), jnp.int32), jax.ShapeDtypeStruct((t, LANES), F32),
                   jax.ShapeDtypeStruct((8, ne), F32)],
        compiler_params=_cparams(("arbitrary",)),
        name="route",
    )(logits, bias.reshape(1, ne))


def _rank_kernel(eidx_ref, pstart_ref, ltri_ref, dest_ref, run_sc):
    @pl.when(pl.program_id(0) == 0)
    def _():
        run_sc[...] = jnp.zeros(run_sc.shape, F32)

    e = eidx_ref[...]
    tm = e.shape[0]
    ne = pstart_ref.shape[1]
    lane = lax.broadcasted_iota(jnp.int32, (1, ne), 1)
    onehot = [lane == e[:, k:k + 1] for k in range(TOP_K)]
    cnt = jnp.zeros((tm, ne), F32)
    for oh in onehot:
        cnt = cnt + jnp.where(oh, 1.0, 0.0)
    before = _dot(ltri_ref[...], cnt.astype(BF16)) + (run_sc[...] + pstart_ref[...])
    out_lane = lax.broadcasted_iota(jnp.int32, (1, LANES), 1)
    dest = jnp.zeros((tm, LANES), F32)
    for k, oh in enumerate(onehot):
        dk = jnp.sum(jnp.where(oh, before, 0.0), axis=-1, keepdims=True)
        dest = jnp.where(out_lane == k, dk, dest)
    dest_ref[...] = dest.astype(jnp.int32)
    run_sc[...] = run_sc[...] + jnp.sum(cnt, axis=0, keepdims=True)


def _rank(eidx128, pad_start):
    t = eidx128.shape[0]
    tm = TM_ROWS
    ne = pad_start.shape[0]
    ltri = jnp.asarray(np.tril(np.ones((tm, tm), np.float32), -1), BF16)
    return pl.pallas_call(
        _rank_kernel,
        grid=(t // tm,),
        in_specs=[pl.BlockSpec((tm, LANES), lambda i: (i, 0)), pl.BlockSpec((1, ne), lambda i: (0, 0)),
                  pl.BlockSpec((tm, tm), lambda i: (0, 0))],
        out_specs=pl.BlockSpec((tm, LANES), lambda i: (i, 0)),
        out_shape=jax.ShapeDtypeStruct((t, LANES), jnp.int32),
        scratch_shapes=[pltpu.VMEM((1, ne), F32)],
        compiler_params=_cparams(("arbitrary",)),
        name="rank",
    )(eidx128, pad_start.astype(F32).reshape(1, ne), ltri)


def _row_copy(src_ref, src_row, dst_ref, dst_row, sem):
    return pltpu.make_async_copy(src_ref.at[pl.ds(src_row, 1), :], dst_ref.at[pl.ds(dst_row, 1), :], sem)


def _dispatch_kernel(dest_ref, h_ref, xs_init_ref, xs_ref, sem):
    del xs_init_ref
    n = dest_ref.shape[0]
    log2k = TOP_K.bit_length() - 1

    def copy(j):
        return _row_copy(h_ref, lax.shift_right_logical(j, log2k), xs_ref, dest_ref[j], sem)

    def issue(j, carry):
        copy(j).start()
        return carry

    def drain(j, carry):
        copy(j).wait()
        return carry

    lax.fori_loop(0, n, issue, 0, unroll=8)
    lax.fori_loop(0, n, drain, 0, unroll=8)


def _dispatch(dest_flat, h2, rows):
    t, d = h2.shape
    tm = DISPATCH_TM
    return pl.pallas_call(
        _dispatch_kernel,
        grid=(t // tm,),
        in_specs=[pl.BlockSpec((tm * TOP_K,), lambda i: (i,), memory_space=pltpu.SMEM),
                  pl.BlockSpec((tm, d), lambda i: (i, 0)),
                  pl.BlockSpec(memory_space=pl.ANY)],
        out_specs=pl.BlockSpec(memory_space=pl.ANY),
        out_shape=jax.ShapeDtypeStruct((rows, d), F32),
        scratch_shapes=[pltpu.SemaphoreType.DMA(())],
        input_output_aliases={2: 0},
        compiler_params=_cparams(("arbitrary",)),
        name="dispatch",
    )(dest_flat, h2, jnp.zeros((rows, d), F32))


def _experts_kernel(blk_e_ref, nused_ref, x_ref, wg_ref, wu_ref, wd_ref, y_ref):
    i = pl.program_id(0)

    @pl.when(i < nused_ref[0])
    def _():
        x = x_ref[...].astype(BF16)
        hg = _dot(x, wg_ref[0].astype(BF16))
        hu = _dot(x, wu_ref[0].astype(BF16))
        hb = (hg * jax.nn.sigmoid(hg)) * hu
        y_ref[...] = _dot(hb.astype(BF16), wd_ref[0].astype(BF16))

    @pl.when(i >= nused_ref[0])
    def _():
        y_ref[...] = jnp.zeros(y_ref.shape, y_ref.dtype)


def _experts(blk_e, nused, xs, w_gate, w_up, w_down):
    rows, d = xs.shape
    bm = MOE_BM
    nb = rows // bm
    de = w_gate.shape[2]
    xmap = lambda i, be, nu: (jnp.minimum(i, nu[0] - 1), 0)
    wmap = lambda i, be, nu: (be[i], 0, 0)
    grid_spec = pltpu.PrefetchScalarGridSpec(
        num_scalar_prefetch=2,
        grid=(nb,),
        in_specs=[pl.BlockSpec((bm, d), xmap),
                  pl.BlockSpec((1, d, de), wmap), pl.BlockSpec((1, d, de), wmap), pl.BlockSpec((1, de, d), wmap)],
        out_specs=pl.BlockSpec((bm, d), lambda i, be, nu: (i, 0)),
    )
    return pl.pallas_call(
        _experts_kernel,
        grid_spec=grid_spec,
        out_shape=jax.ShapeDtypeStruct((rows, d), F32),
        compiler_params=_cparams(("arbitrary",)),
        name="experts",
    )(blk_e, nused, xs, w_gate, w_up, w_down)


def _block_tables(counts, n_tokens):
    bm = MOE_BM
    nb = -(-(n_tokens * TOP_K + N_EXPERTS * (bm - 1)) // bm)
    padded = (counts + bm - 1) // bm * bm
    pad_end = jnp.cumsum(padded)
    pad_start = pad_end - padded
    nused = (pad_end[-1] // bm).astype(jnp.int32).reshape(1)
    first_row = jnp.arange(nb, dtype=jnp.int32) * bm
    blk_e = jnp.minimum(jnp.sum((pad_end[None, :] <= first_row[:, None]).astype(jnp.int32), axis=1), N_EXPERTS - 1)
    return blk_e.astype(jnp.int32), nused, pad_start, nb * bm


def _final_kernel(dest_ref, x1_ref, h2_ref, wts_ref, y_hbm, g2_ref, wsg_ref, wsu_ref, wsd_ref, fg_ref, o_ref,
                  ybuf, sem):
    n = dest_ref.shape[0]
    log2k = TOP_K.bit_length() - 1

    def copy(j):
        k = jnp.bitwise_and(j, TOP_K - 1)
        return _row_copy(y_hbm, dest_ref[j], ybuf.at[k], lax.shift_right_logical(j, log2k), sem)

    def issue(j, carry):
        copy(j).start()
        return carry

    def drain(j, carry):
        copy(j).wait()
        return carry

    lax.fori_loop(0, n, issue, 0, unroll=8)
    h = h2_ref[...].astype(BF16)
    hg = _dot(h, wsg_ref[...])
    hu = _dot(h, wsu_ref[...])
    y = _dot(((hg * jax.nn.sigmoid(hg)) * hu).astype(BF16), wsd_ref[...])
    lax.fori_loop(0, n, drain, 0, unroll=8)
    w = wts_ref[...]
    for k in range(TOP_K):
        y = y + w[:, k:k + 1] * ybuf[k]
    x2 = x1_ref[...] + g2_ref[0] * y
    o_ref[...] = (x2 * lax.rsqrt(jnp.mean(x2 * x2, axis=-1, keepdims=True) + RMS_EPS)) * fg_ref[...]


def _final(dest_flat, x1, h2, wts128, y_rows, g2, wsg, wsu, wsd, fg, seq):
    t, d = x1.shape
    tm = DISPATCH_TM
    per_b = seq // tm
    row = lambda i: (i, 0)
    full = lambda i: (0, 0)
    return pl.pallas_call(
        _final_kernel,
        grid=(t // tm,),
        in_specs=[pl.BlockSpec((tm * TOP_K,), lambda i: (i,), memory_space=pltpu.SMEM),
                  pl.BlockSpec((tm, d), row), pl.BlockSpec((tm, d), row), pl.BlockSpec((tm, LANES), row),
                  pl.BlockSpec(memory_space=pl.ANY),
                  pl.BlockSpec((1, 1, d), lambda i: (i // per_b, 0, 0)),
                  pl.BlockSpec(wsg.shape, full), pl.BlockSpec(wsu.shape, full), pl.BlockSpec(wsd.shape, full),
                  pl.BlockSpec((1, d), full)],
        out_specs=pl.BlockSpec((tm, d), row),
        out_shape=jax.ShapeDtypeStruct((t, d), F32),
        scratch_shapes=[pltpu.VMEM((TOP_K, tm, d), F32), pltpu.SemaphoreType.DMA(())],
        compiler_params=_cparams(("arbitrary",)),
        name="final",
    )(dest_flat, x1, h2, wts128, y_rows, g2, wsg, wsu, wsd, fg)


def _with_features(base, feats):
    f = jnp.broadcast_to(feats, base.shape[:-1] + feats.shape[-1:])
    padw = LANES - base.shape[-1] - feats.shape[-1]
    return jnp.concatenate([base, f, jnp.zeros(base.shape[:-1] + (padw,), BF16)], axis=-1)


def _key_features(seq):
    pos = np.arange(seq)
    return _pos_features(64.0 * (pos // 64), pos % 64)


def _own_block(seq, block):
    pos = np.arange(seq)
    return jnp.asarray((pos[:, None] // block == np.arange(seq // block)[None, :]) * BIG, BF16)


def _mixer_a(qa, kva, ga, bsz, seq, cmp_pe_k, cmp_w1_k, cmp_b1_k, cmp_w2_k, cmp_pe_v, cmp_w1_v, cmp_b1_v, cmp_w2_v):
    hkv, grp, dh = NSA_KV_HEADS, NSA_GROUP, HEAD_DIM
    slopes_a, _ = _alibi_slopes()
    kv6 = kva.reshape(bsz, seq, 6, hkv, dh).transpose(2, 0, 3, 1, 4)
    n16 = seq // CMP_STRIDE
    cmp_in = kv6[0:2].reshape(2, bsz * hkv, n16, CMP_STRIDE * dh)
    pe = jnp.stack([cmp_pe_k, cmp_pe_v]).reshape(2, 1, CMP_LEN * dh)
    pe = jnp.broadcast_to(pe, (2, 8, CMP_LEN * dh))
    kvc = _compress(cmp_in, pe, jnp.stack([cmp_w1_k, cmp_w1_v]),
                    jnp.stack([cmp_b1_k, cmp_b1_v]).reshape(2, 1, CMP_HIDDEN),
                    jnp.stack([cmp_w2_k, cmp_w2_v]))
    kvc = kvc.reshape(2, bsz, hkv, n16, dh).astype(BF16)
    cidx = np.arange(n16)
    kc_aug = _with_features(kvc[0], _pos_features(64.0 * (cidx // 4), 16.0 * (cidx % 4) + (CMP_LEN - 1) / 2))
    key_feats = _key_features(seq)
    ns = seq // SEL_BLOCK
    ks_aug = jnp.concatenate([_with_features(kv6[2], key_feats),
                              jnp.broadcast_to(_own_block(seq, SEL_BLOCK), (bsz, hkv, seq, ns))], axis=-1)
    kw_aug = _with_features(kv6[4], key_feats)
    qa5 = qa.reshape(bsz, seq, hkv, grp, dh).transpose(0, 2, 3, 1, 4)
    qa_aug = _with_features(qa5, _slope_features(slopes_a).reshape(1, hkv, grp, 1, 6))
    ga_h = ga[:, :NSA_GATES].reshape(bsz, seq, hkv, 3 * grp).transpose(0, 2, 1, 3)
    ga_h = jnp.pad(ga_h, ((0, 0), (0, 0), (0, 0), (0, LANES - 3 * grp)))
    return _nsa(qa_aug, kc_aug, kvc[1], ks_aug, _with_ones(kv6[3]), kw_aug, _with_ones(kv6[5]), ga_h,
                _importance_matrix(n16, ns))


def _mixer_b(qkvb, bsz, seq):
    hb, dh = MOBA_HEADS, HEAD_DIM
    _, slopes_b = _alibi_slopes()
    qkv = qkvb.reshape(bsz, seq, 3, hb, dh).transpose(2, 0, 3, 1, 4)
    nbk = seq // MOBA_BLOCK
    qb_aug = _with_features(qkv[0], _slope_features(slopes_b).reshape(1, hb, 1, 6))
    kb_aug = _with_features(qkv[1], _key_features(seq))
    kb_aug = jnp.concatenate([kb_aug[..., :MOBA_SEL0],
                              jnp.broadcast_to(_own_block(seq, MOBA_BLOCK), (bsz, hb, seq, nbk))], axis=-1)
    km = _kmean(qkv[1]).astype(BF16)
    km_aug = jnp.pad(km, ((0, 0), (0, 0), (MOBA_SEL0, LANES - MOBA_SEL0 - nbk), (0, LANES - dh)))
    return _moba(qb_aug, km_aug, kb_aug, _with_ones(qkv[2]))


def _layer(x, c, w_ada, b_ada, norm1_g, w_in, cmp_pe_k, cmp_w1_k, cmp_b1_k, cmp_w2_k,
           cmp_pe_v, cmp_w1_v, cmp_b1_v, cmp_w2_v, w_up_a, w_up_b, w_out, norm2_g,
           w_router, router_bias, w_e_gate, w_e_up, w_e_down, w_s_gate, w_s_up, w_s_down):
    bsz, seq, d = x.shape
    t = bsz * seq
    x2 = x.reshape(t, d)
    slopes_a, slopes_b = _alibi_slopes()

    ada = _ada(c, w_ada, b_ada)
    sh1, sc1, g1, sh2, sc2, g2 = [a[:, None, :] for a in jnp.split(ada, 6, axis=-1)]

    qa, kva, ga, qkvb, gates = _inproj(x2, sc1, sh1, norm1_g.reshape(1, d), _pack_w_in(w_in), seq)
    o_a = _mixer_a(qa, kva, ga, bsz, seq, cmp_pe_k, cmp_w1_k, cmp_b1_k, cmp_w2_k,
                   cmp_pe_v, cmp_w1_v, cmp_b1_v, cmp_w2_v)
    o_b = _mixer_b(qkvb, bsz, seq)
    x1, h2, logits = _merge(o_a.reshape(t, NSA_Q), o_b.reshape(t, MOBA_W), gates, x2, g1, sc2, sh2,
                            norm2_g.reshape(1, d), w_up_a.astype(BF16), w_up_b.astype(BF16), w_out.astype(BF16),
                            w_router, seq)

    eidx128, wts128, counts8 = _route(logits, router_bias)
    blk_e, nused, pad_start, rows = _block_tables(counts8[0].astype(jnp.int32), t)
    dest_flat = _rank(eidx128, pad_start)[:, :TOP_K].reshape(-1)
    xs = _dispatch(dest_flat, h2, rows)
    y_rows = _experts(blk_e, nused, xs, w_e_gate, w_e_up, w_e_down)
    return dest_flat, x1, h2, wts128, y_rows, g2


def kernel(x, c, w_ada, b_ada, norm1_g, w_in, cmp_pe_k, cmp_w1_k, cmp_b1_k, cmp_w2_k, cmp_pe_v, cmp_w1_v, cmp_b1_v, cmp_w2_v, w_up_a, w_up_b, w_out, norm2_g, w_router, router_bias, w_e_gate, w_e_up, w_e_down, w_s_gate, w_s_up, w_s_down, final_g):
    bsz, seq, d = x.shape
    depth = w_ada.shape[0]
    assert depth == 1, "the final-norm fusion below assumes a single layer"
    l = 0
    moe = _layer(x, c, w_ada[l], b_ada[l], norm1_g[l], w_in[l], cmp_pe_k[l], cmp_w1_k[l], cmp_b1_k[l],
                 cmp_w2_k[l], cmp_pe_v[l], cmp_w1_v[l], cmp_b1_v[l], cmp_w2_v[l], w_up_a[l], w_up_b[l],
                 w_out[l], norm2_g[l], w_router[l], router_bias[l], w_e_gate[l], w_e_up[l], w_e_down[l],
                 w_s_gate[l], w_s_up[l], w_s_down[l])
    out = _final(*moe, w_s_gate[l].astype(BF16), w_s_up[l].astype(BF16), w_s_down[l].astype(BF16),
                 final_g.reshape(1, d), seq)
    return out.reshape(bsz, seq, d)
```
